```python
import math
import jax
import jax.numpy as jnp
from jax import lax
import numpy as np

D_MODEL = 2048
BATCH = 2
SEQ = 4096
DEPTH = 2
DEC_BATCH = 128
DEC_SEQ = 4
PAST_LEN = 2048
PAGE_SIZE = 128

N_A_LAYERS = DEPTH // 2
N_B_LAYERS = DEPTH - N_A_LAYERS
CONV_WIDTH = 31
N_HEADS = 16
HEAD_DIM = 128
KV_HEADS = 4
GROUP = N_HEADS // KV_HEADS
N_BRANCH = 3
N_KV_SLOTS = 6
CMP_LEN = 32
CMP_STRIDE = 16
CMP_HIDDEN = 256
SEL_BLOCK = 64
TOP_N = 16
WINDOW = 512
WIN_BLOCK = 128
SEL_QCHUNK = 64
FFN_HIDDEN = ((8 * D_MODEL + 768 - 1) // 768) * 256
ROPE_THETA = 10000.0
FORCE_BONUS = 1.0e4
EPS = 1e-6

kernel_name = "yoco_conformer_nsa_decoder_step"


def rmsnorm(x, g):
    xf = x.astype(jnp.float32)
    y = xf * lax.rsqrt(jnp.mean(xf * xf, axis=-1, keepdims=True) + EPS)
    return (y * g.astype(jnp.float32)).astype(x.dtype)


def layernorm(x, g, b):
    xf = x.astype(jnp.float32)
    mu = jnp.mean(xf, axis=-1, keepdims=True)
    xc = xf - mu
    y = xc * lax.rsqrt(jnp.mean(xc * xc, axis=-1, keepdims=True) + EPS)
    return (y * g.astype(jnp.float32) + b.astype(jnp.float32)).astype(x.dtype)


def rope(x, pos):
    half = HEAD_DIM // 2
    inv = ROPE_THETA ** (-jnp.arange(half, dtype=jnp.float32) / half)
    ang = pos.astype(jnp.float32)[:, None] * inv[None, :]
    shape = (pos.shape[0],) + (1,) * (x.ndim - 3) + (half,)
    cos = jnp.cos(ang).reshape(shape)
    sin = jnp.sin(ang).reshape(shape)
    xf = x.astype(jnp.float32)
    x1, x2 = xf[..., :half], xf[..., half:]
    return jnp.concatenate([x1 * cos - x2 * sin, x2 * cos + x1 * sin], axis=-1).astype(x.dtype)


def masked_softmax(s, mask):
    s = jnp.where(mask, s.astype(jnp.float32), -jnp.inf)
    m = jnp.max(s, axis=-1, keepdims=True)
    m = jnp.where(jnp.isfinite(m), m, 0.0)
    p = jnp.where(mask, jnp.exp(s - m), 0.0)
    return p / jnp.maximum(jnp.sum(p, axis=-1, keepdims=True), 1e-30)


def swiglu(h, w_in, w_out):
    a = h @ w_in
    return (jax.nn.silu(a[..., :FFN_HIDDEN]) * a[..., FFN_HIDDEN:]) @ w_out


def conv_module(h, buf, w1, b1, wdw, bdw, ln_g, ln_b, w2, b2):
    a = h @ w1 + b1
    u = a[..., :D_MODEL] * jax.nn.sigmoid(a[..., D_MODEL:])
    full = jnp.concatenate([buf, u], axis=1)
    y = lax.conv_general_dilated(full, wdw[:, None, :], window_strides=(1,), padding="VALID",
                                 dimension_numbers=("NWC", "WIO", "NWC"), feature_group_count=D_MODEL)
    y = jax.nn.silu(layernorm(y + bdw, ln_g, ln_b))
    return y @ w2 + b2, full[:, -(CONV_WIDTH - 1):]


def gather_pages(cache, page_table):
    g = cache[page_table]
    return g.reshape(page_table.shape[0], page_table.shape[1] * cache.shape[1], cache.shape[2], cache.shape[3])


def compress(k_full, w1, pe, w2, b2):
    B, L, G, D = k_full.shape
    n_chunk = L // CMP_STRIDE
    ch = k_full[:, : n_chunk * CMP_STRIDE].reshape(B, n_chunk, CMP_STRIDE, G, D)
    first = jnp.einsum("bcsgd,sdh->bcgh", ch, w1[:CMP_STRIDE])
    second = jnp.einsum("bcsgd,sdh->bcgh", ch, w1[CMP_STRIDE:])
    pos_term = jnp.einsum("sd,sdh->h", pe, w1)
    hid = jax.nn.silu(first[:, :-1] + second[:, 1:] + pos_term)
    return hid @ w2 + b2


def cmp_attend(q, kc, vc, pos):
    n = kc.shape[1]
    s = jnp.einsum("btgrd,bngd->bgrtn", q, kc) * (HEAD_DIM ** -0.5)
    end = jnp.arange(n, dtype=jnp.int32) * CMP_STRIDE + (CMP_LEN - 1)
    mask = end[None, :] <= pos[:, None]
    p = masked_softmax(s, mask)
    o = jnp.einsum("bgrtn,bngd->btgrd", p.astype(vc.dtype), vc)
    return o, p


def select_blocks(p_cmp, pos, L):
    n_cmp = p_cmp.shape[-1]
    n_sel = -(-L // SEL_BLOCK)
    ci = jnp.arange(n_cmp, dtype=jnp.int32) * CMP_STRIDE
    sj = jnp.arange(n_sel, dtype=jnp.int32) * SEL_BLOCK
    overlap = ((ci[:, None] < sj[None, :] + SEL_BLOCK) & (ci[:, None] + CMP_LEN > sj[None, :])).astype(jnp.float32)
    imp = jnp.einsum("bgrtn,ns->bgts", p_cmp, overlap)
    blk = jnp.arange(n_sel, dtype=jnp.int32)[None, :]
    cur = (pos // SEL_BLOCK)[:, None]
    forced = (blk == 0) | (blk == cur) | (blk == cur - 1)
    causal = sj[None, :] <= pos[:, None]
    score = jnp.where(causal, imp + FORCE_BONUS * forced.astype(jnp.float32), -jnp.inf)
    vals, idx = lax.top_k(score, min(TOP_N, n_sel))
    return idx, jnp.isfinite(vals)


def sel_attend(q, k, v, idx, valid, pos, chunk):
    B, T, G, R, D = q.shape
    L = k.shape[1]
    K = idx.shape[-1]
    nc = T // chunk
    qs = q.reshape(B, nc, chunk, G, R, D).transpose(1, 0, 2, 3, 4, 5)
    ids = idx.reshape(B, G, nc, chunk, K).transpose(2, 0, 1, 3, 4)
    vals = valid.reshape(B, G, nc, chunk, K).transpose(2, 0, 1, 3, 4)
    ps = pos.reshape(nc, chunk)
    gather = jax.vmap(jax.vmap(lambda m, i: m[i], in_axes=(1, 0)), in_axes=(0, 0))
    offs = jnp.arange(SEL_BLOCK, dtype=jnp.int32)

    def body(args):
        qc, ic, vc_, pc = args
        tok = ic[..., None] * SEL_BLOCK + offs
        mask = vc_[..., None] & (tok <= pc[None, None, :, None, None])
        tok = jnp.minimum(tok, L - 1)
        kg = gather(k, tok)
        vg = gather(v, tok)
        s = jnp.einsum("bcgrd,bgcksd->bgrcks", qc, kg) * (HEAD_DIM ** -0.5)
        s = s.reshape(B, G, R, chunk, K * SEL_BLOCK)
        p = masked_softmax(s, mask.reshape(B, G, 1, chunk, K * SEL_BLOCK))
        return jnp.einsum("bgrcn,bgcnd->bcgrd", p.astype(vg.dtype), vg.reshape(B, G, chunk, K * SEL_BLOCK, D))

    out = lax.map(body, (qs, ids, vals, ps))
    return out.transpose(1, 0, 2, 3, 4, 5).reshape(B, T, G, R, D)


def window_banded(q, k, v):
    B, T, G, R, D = q.shape
    nb = T // WIN_BLOCK
    span = WINDOW + WIN_BLOCK
    kp = jnp.pad(k, ((0, 0), (WINDOW, 0), (0, 0), (0, 0)))
    vp = jnp.pad(v, ((0, 0), (WINDOW, 0), (0, 0), (0, 0)))
    idx = jnp.arange(nb)[:, None] * WIN_BLOCK + jnp.arange(span)[None, :]
    kb = kp[:, idx]
    vb = vp[:, idx]
    qb = q.reshape(B, nb, WIN_BLOCK, G, R, D)
    s = jnp.einsum("bnqgrd,bnkgd->bgrnqk", qb, kb) * (HEAD_DIM ** -0.5)
    base = jnp.arange(nb, dtype=jnp.int32)[:, None, None] * WIN_BLOCK
    qpos = base + jnp.arange(WIN_BLOCK, dtype=jnp.int32)[None, :, None]
    kpos = base + jnp.arange(span, dtype=jnp.int32)[None, None, :] - WINDOW
    mask = (kpos >= 0) & (kpos <= qpos) & (qpos - kpos < WINDOW)
    p = masked_softmax(s, mask)
    o = jnp.einsum("bgrnqk,bnkgd->bnqgrd", p.astype(vb.dtype), vb)
    return o.reshape(B, T, G, R, D)


def window_direct(q, k, v, kpos, pos):
    s = jnp.einsum("btgrd,bkgd->bgrtk", q, k) * (HEAD_DIM ** -0.5)
    mask = (kpos[None, :] <= pos[:, None]) & (pos[:, None] - kpos[None, :] < WINDOW)
    p = masked_softmax(s, mask)
    return jnp.einsum("bgrtk,bkgd->btgrd", p.astype(v.dtype), v)


def nsa_mixer(h, pos, kv, sel_chunk, w_qg, w_o, is_prompt):
    B, T, _ = h.shape
    qg = h @ w_qg
    q = rope(qg[..., : N_HEADS * HEAD_DIM].reshape(B, T, KV_HEADS, GROUP, HEAD_DIM), pos)
    gates = jax.nn.sigmoid(qg[..., N_HEADS * HEAD_DIM:].astype(jnp.float32))
    gates = gates.reshape(B, T, KV_HEADS, GROUP, N_BRANCH).astype(h.dtype)
    o_cmp, p_cmp = cmp_attend(q, kv["k_cmp"], kv["v_cmp"], pos)
    idx, valid = select_blocks(p_cmp, pos, kv["k_sel"].shape[1])
    o_sel = sel_attend(q, kv["k_sel"], kv["v_sel"], idx, valid, pos, sel_chunk)
    if is_prompt:
        o_win = window_banded(q, kv["k_win"], kv["v_win"])
    else:
        o_win = window_direct(q, kv["k_win"], kv["v_win"], kv["kpos_win"], pos)
    o = gates[..., 0:1] * o_cmp + gates[..., 1:2] * o_sel + gates[..., 2:3] * o_win
    return o.reshape(B, T, N_HEADS * HEAD_DIM) @ w_o


def shared_kv(x, pos, past, prm):
    B, T, _ = x.shape
    kvp = (rmsnorm(x, prm["kv_norm_g"]) @ prm["w_kv"]).reshape(B, T, N_KV_SLOTS, KV_HEADS, HEAD_DIM)
    kc = rope(kvp[:, :, 0], pos)
    vc = kvp[:, :, 1]
    ks = rope(kvp[:, :, 2], pos)
    vs = kvp[:, :, 3]
    kw = rope(kvp[:, :, 4], pos)
    vw = kvp[:, :, 5]
    if past is None:
        kc_f, vc_f, ks_f, vs_f = kc, vc, ks, vs
        kw_all, vw_all, kpos_w = kw, vw, pos
        nw = min(WINDOW, T)
    else:
        pt = past["page_table"]
        kc_f = jnp.concatenate([gather_pages(past["k_cmp"], pt), kc], axis=1)
        vc_f = jnp.concatenate([gather_pages(past["v_cmp"], pt), vc], axis=1)
        ks_f = jnp.concatenate([gather_pages(past["k_sel"], pt), ks], axis=1)
        vs_f = jnp.concatenate([gather_pages(past["v_sel"], pt), vs], axis=1)
        nw = past["k_win"].shape[1]
        past_len = pt.shape[1] * past["k_sel"].shape[1]
        kw_all = jnp.concatenate([past["k_win"], kw], axis=1)
        vw_all = jnp.concatenate([past["v_win"], vw], axis=1)
        kpos_w = past_len - nw + jnp.arange(nw + T, dtype=jnp.int32)
    k_cmp = compress(kc_f, prm["cmp_w1"][0], prm["cmp_pe"][0], prm["cmp_w2"][0], prm["cmp_b2"][0])
    v_cmp = compress(vc_f, prm["cmp_w1"][1], prm["cmp_pe"][1], prm["cmp_w2"][1], prm["cmp_b2"][1])
    kv = dict(k_cmp=k_cmp, v_cmp=v_cmp, k_sel=ks_f, v_sel=vs_f, k_win=kw_all, v_win=vw_all, kpos_win=kpos_w)
    rows = (kc, vc, ks, vs, kw_all[:, -nw:], vw_all[:, -nw:])
    return kv, rows


def trunk(x, pos, conv_bufs, past, sel_chunk, prm):
    new_conv = []
    kv = None
    rows = None
    for l in range(DEPTH):
        g = prm["norm_g"][l]
        h = rmsnorm(x, g[0])
        if l < N_A_LAYERS:
            mix, buf = conv_module(h, conv_bufs[l], prm["conv_w1"][l], prm["conv_b1"][l], prm["conv_wdw"][l],
                                   prm["conv_bdw"][l], prm["conv_ln_g"][l], prm["conv_ln_b"][l],
                                   prm["conv_w2"][l], prm["conv_b2"][l])
            new_conv.append(buf)
        else:
            j = l - N_A_LAYERS
            mix = nsa_mixer(h, pos, kv, sel_chunk, prm["nsa_w_qg"][j], prm["nsa_w_o"][j], past is None)
        x = x + rmsnorm(mix, g[1])
        x = x + rmsnorm(swiglu(rmsnorm(x, g[2]), prm["ffn_w_in"][l], prm["ffn_w_out"][l]), g[3])
        if l == N_A_LAYERS - 1:
            kv, rows = shared_kv(x, pos, past, prm)
    return x, jnp.stack(new_conv), rows


def setup_inputs(seed: int = 0) -> dict:
    key = jax.random.key(seed)
    ks = jax.random.split(key, 40)

    def nrm(k, shape, scale):
        return jax.random.normal(k, shape, jnp.float32) * scale

    n_pages = PAST_LEN // PAGE_SIZE
    n_used = DEC_BATCH * n_pages
    n_pool = n_used + n_used // 4
    w_buf = min(WINDOW, PAST_LEN)
    page_table = jax.random.permutation(ks[0], n_pool)[:n_used].reshape(DEC_BATCH, n_pages).astype(jnp.int32)
    cshape = (n_pool, PAGE_SIZE, KV_HEADS, HEAD_DIM)
    wshape = (DEC_BATCH, w_buf, KV_HEADS, HEAD_DIM)
    qg_cols = N_HEADS * HEAD_DIM + N_BRANCH * N_HEADS
    return {
        "x_prompt": nrm(ks[1], (BATCH, SEQ, D_MODEL), 1.0),
        "x_sample": nrm(ks[2], (DEC_BATCH, DEC_SEQ, D_MODEL), 1.0),
        "cache_k_cmp": nrm(ks[3], cshape, 1.0),
        "cache_v_cmp": nrm(ks[4], cshape, 1.0),
        "cache_k_sel": nrm(ks[5], cshape, 1.0),
        "cache_v_sel": nrm(ks[6], cshape, 1.0),
        "state_k_win": nrm(ks[7], wshape, 1.0),
        "state_v_win": nrm(ks[8], wshape, 1.0),
        "state_conv": nrm(ks[9], (N_A_LAYERS, DEC_BATCH, CONV_WIDTH - 1, D_MODEL), 0.5),
        "page_table": page_table,
        "norm_g": 1.0 + nrm(ks[10], (DEPTH, 4, D_MODEL), 0.05),
        "conv_w1": nrm(ks[11], (N_A_LAYERS, D_MODEL, 2 * D_MODEL), D_MODEL ** -0.5),
        "conv_b1": nrm(ks[12], (N_A_LAYERS, 2 * D_MODEL), 0.02),
        "conv_wdw": nrm(ks[13], (N_A_LAYERS, CONV_WIDTH, D_MODEL), CONV_WIDTH ** -0.5),
        "conv_bdw": nrm(ks[14], (N_A_LAYERS, D_MODEL), 0.02),
        "conv_ln_g": 1.0 + nrm(ks[15], (N_A_LAYERS, D_MODEL), 0.05),
        "conv_ln_b": nrm(ks[16], (N_A_LAYERS, D_MODEL), 0.02),
        "conv_w2": nrm(ks[17], (N_A_LAYERS, D_MODEL, D_MODEL), D_MODEL ** -0.5),
        "conv_b2": nrm(ks[18], (N_A_LAYERS, D_MODEL), 0.02),
        "kv_norm_g": 1.0 + nrm(ks[19], (D_MODEL,), 0.05),
        "w_kv": nrm(ks[20], (D_MODEL, N_KV_SLOTS * KV_HEADS * HEAD_DIM), D_MODEL ** -0.5),
        "cmp_w1": nrm(ks[21], (2, CMP_LEN, HEAD_DIM, CMP_HIDDEN), (CMP_LEN * HEAD_DIM) ** -0.5),
        "cmp_pe": nrm(ks[22], (2, CMP_LEN, HEAD_DIM), 0.1),
        "cmp_w2": nrm(ks[23], (2, CMP_HIDDEN, HEAD_DIM), CMP_HIDDEN ** -0.5),
        "cmp_b2": nrm(ks[24], (2, HEAD_DIM), 0.02),
        "nsa_w_qg": nrm(ks[25], (N_B_LAYERS, D_MODEL, qg_cols), D_MODEL ** -0.5),
        "nsa_w_o": nrm(ks[26], (N_B_LAYERS, N_HEADS * HEAD_DIM, D_MODEL), (N_HEADS * HEAD_DIM) ** -0.5),
        "ffn_w_in": nrm(ks[27], (DEPTH, D_MODEL, 2 * FFN_HIDDEN), D_MODEL ** -0.5),
        "ffn_w_out": nrm(ks[28], (DEPTH, FFN_HIDDEN, D_MODEL), FFN_HIDDEN ** -0.5),
    }


def reference(x_prompt, x_sample, cache_k_cmp, cache_v_cmp, cache_k_sel, cache_v_sel, state_k_win, state_v_win,
              state_conv, page_table, norm_g, conv_w1, conv_b1, conv_wdw, conv_bdw, conv_ln_g, conv_ln_b, conv_w2,
              conv_b2, kv_norm_g, w_kv, cmp_w1, cmp_pe, cmp_w2, cmp_b2, nsa_w_qg, nsa_w_o, ffn_w_in, ffn_w_out):
    prm = dict(norm_g=norm_g, conv_w1=conv_w1, conv_b1=conv_b1, conv_wdw=conv_wdw, conv_bdw=conv_bdw,
               conv_ln_g=conv_ln_g, conv_ln_b=conv_ln_b, conv_w2=conv_w2, conv_b2=conv_b2, kv_norm_g=kv_norm_g,
               w_kv=w_kv, cmp_w1=cmp_w1, cmp_pe=cmp_pe, cmp_w2=cmp_w2, cmp_b2=cmp_b2, nsa_w_qg=nsa_w_qg,
               nsa_w_o=nsa_w_o, ffn_w_in=ffn_w_in, ffn_w_out=ffn_w_out)
    t_p = x_prompt.shape[1]
    t_s = x_sample.shape[1]
    past_len = page_table.shape[1] * cache_k_sel.shape[1]
    pos_p = jnp.arange(t_p, dtype=jnp.int32)
    pos_s = past_len + jnp.arange(t_s, dtype=jnp.int32)
    zero_conv = jnp.zeros((N_A_LAYERS, x_prompt.shape[0], CONV_WIDTH - 1, D_MODEL), x_prompt.dtype)
    y_prompt, conv_p, rows_p = trunk(x_prompt, pos_p, zero_conv, None, min(SEL_QCHUNK, t_p), prm)
    past = dict(k_cmp=cache_k_cmp, v_cmp=cache_v_cmp, k_sel=cache_k_sel, v_sel=cache_v_sel,
                k_win=state_k_win, v_win=state_v_win, page_table=page_table)
    y_sample, conv_s, rows_s = trunk(x_sample, pos_s, state_conv, past, 1, prm)
    kc_p, vc_p, ks_p, vs_p, kw_p, vw_p = rows_p
    kc_s, vc_s, ks_s, vs_s, kw_s, vw_s = rows_s
    return (y_prompt, y_sample, conv_p, kc_p, vc_p, ks_p, vs_p, kw_p, vw_p,
            conv_s, kc_s, vc_s, ks_s, vs_s, kw_s, vw_s)
```

```python
import functools

import jax
import jax.numpy as jnp
from jax import lax
from jax.experimental import pallas as pl
from jax.experimental.pallas import tpu as pltpu

D_MODEL = 2048
CONV_WIDTH = 31
CONV_HALO = CONV_WIDTH - 1
N_HEADS = 16
HEAD_DIM = 128
KV_HEADS = 4
GROUP = N_HEADS // KV_HEADS
N_BRANCH = 3
N_KV_SLOTS = 6
KV_COLS = KV_HEADS * HEAD_DIM
CMP_LEN = 32
CMP_STRIDE = 16
CMP_HIDDEN = 256
SEL_BLOCK = 64
TOP_N = 16
WINDOW = 512
ROPE_THETA = 10000.0
FORCE_BONUS = 1.0e4
EPS = 1e-6
SCALE = HEAD_DIM ** -0.5
SEL_SHIFT = SEL_BLOCK.bit_length() - 1
assert 1 << SEL_SHIFT == SEL_BLOCK

BF = jnp.bfloat16
F32 = jnp.float32
NEG_INF = float("-inf")

VMEM_LIMIT_BYTES = 56 * 1024 * 1024
ROW_TILE = 512
KEY_TILE = 128
Q_TILE = 128


def _params(*sem):
    return pltpu.CompilerParams(dimension_semantics=sem, vmem_limit_bytes=VMEM_LIMIT_BYTES)


def _sigmoid(x):
    return 1.0 / (1.0 + jnp.exp(-x))


def _rms(x, g):
    return x * lax.rsqrt(jnp.mean(x * x, axis=-1, keepdims=True) + EPS) * g


def _rope(y, cos, sin_signed):
    return y * cos + pltpu.roll(y, HEAD_DIM // 2, axis=1) * sin_signed


def _split3(x):
    hi = x.astype(BF)
    r1 = x - hi.astype(F32)
    mid = r1.astype(BF)
    lo = (r1 - mid.astype(F32)).astype(BF)
    return hi, mid, lo


def _div(x, n):
    assert n & (n - 1) == 0
    return x >> (n.bit_length() - 1)


def _mod(x, n):
    assert n & (n - 1) == 0
    return x & (n - 1)


def _dot(a, b):
    return jnp.dot(a, b, preferred_element_type=F32)


def _dot_nt(a, b):
    return lax.dot_general(a, b, (((1,), (1,)), ((), ())), preferred_element_type=F32)


def _dot01(x, onehot):
    hi, mid, lo = _split3(x)
    return _dot(hi, onehot) + _dot(mid, onehot) + _dot(lo, onehot)


def _glu_in_kernel(x_ref, g_ref, wa_ref, wb_ref, ba_ref, bb_ref, o_ref, xn_ref):
    @pl.when(pl.program_id(1) == 0)
    def _():
        xn_ref[...] = _rms(x_ref[...], g_ref[...]).astype(BF)

    xn = xn_ref[...]
    a = _dot(xn, wa_ref[...]) + ba_ref[...]
    b = _dot(xn, wb_ref[...]) + bb_ref[...]
    o_ref[...] = a * _sigmoid(b)


def glu_in(x, g, w, b):
    m, d = x.shape
    tn = 512
    nj = d // tn
    return pl.pallas_call(
        _glu_in_kernel,
        out_shape=jax.ShapeDtypeStruct((m, d), F32),
        grid=(m // ROW_TILE, nj),
        in_specs=[
            pl.BlockSpec((ROW_TILE, d), lambda i, j: (i, 0)),
            pl.BlockSpec((1, d), lambda i, j: (0, 0)),
            pl.BlockSpec((d, tn), lambda i, j: (0, j)),
            pl.BlockSpec((d, tn), lambda i, j: (0, j + nj)),
            pl.BlockSpec((1, tn), lambda i, j: (0, j)),
            pl.BlockSpec((1, tn), lambda i, j: (0, j + nj)),
        ],
        out_specs=pl.BlockSpec((ROW_TILE, tn), lambda i, j: (i, j)),
        scratch_shapes=[pltpu.VMEM((ROW_TILE, d), BF)],
        compiler_params=_params("parallel", "arbitrary"),
        name="glu_in",
    )(x, g.reshape(1, d), w, w, b.reshape(1, 2 * d), b.reshape(1, 2 * d))


def _mm_norm_res_kernel(*refs, has_bias):
    if has_bias:
        h_ref, w_ref, b_ref, g_ref, x_ref, o_ref = refs
    else:
        h_ref, w_ref, g_ref, x_ref, o_ref = refs
    y = _dot(h_ref[...].astype(BF), w_ref[...])
    if has_bias:
        y = y + b_ref[...]
    o_ref[...] = x_ref[...] + _rms(y, g_ref[...])


def mm_norm_res(h, w, b, g, x):
    m, k = h.shape
    d = w.shape[1]
    tm = 256
    row = lambda i: (i, 0)
    fixed = lambda i: (0, 0)
    in_specs = [pl.BlockSpec((tm, k), row), pl.BlockSpec((k, d), fixed)]
    args = [h, w]
    if b is not None:
        in_specs.append(pl.BlockSpec((1, d), fixed))
        args.append(b.reshape(1, d))
    in_specs += [pl.BlockSpec((1, d), fixed), pl.BlockSpec((tm, d), row)]
    args += [g.reshape(1, d), x]
    return pl.pallas_call(
        functools.partial(_mm_norm_res_kernel, has_bias=b is not None),
        out_shape=jax.ShapeDtypeStruct((m, d), F32),
        grid=(m // tm,),
        in_specs=in_specs,
        out_specs=pl.BlockSpec((tm, d), row),
        compiler_params=_params("parallel"),
        name="mm_norm_res",
    )(*args)


def _ffn_kernel(x_ref, gi_ref, wa_ref, wb_ref, wo_ref, go_ref, o_ref, xn_ref, acc_ref):
    j = pl.program_id(1)

    @pl.when(j == 0)
    def _():
        xn_ref[...] = _rms(x_ref[...], gi_ref[...]).astype(BF)
        acc_ref[...] = jnp.zeros_like(acc_ref)

    xn = xn_ref[...]
    a = _dot(xn, wa_ref[...])
    b = _dot(xn, wb_ref[...])
    h = (a * _sigmoid(a) * b).astype(BF)
    acc_ref[...] += _dot(h, wo_ref[...])

    @pl.when(j == pl.num_programs(1) - 1)
    def _():
        o_ref[...] = x_ref[...] + _rms(acc_ref[...], go_ref[...])


def ffn(x, g_in, w_in, w_out, g_out):
    m, d = x.shape
    f = w_out.shape[0]
    tf = 512
    nj = f // tf
    return pl.pallas_call(
        _ffn_kernel,
        out_shape=jax.ShapeDtypeStruct((m, d), F32),
        grid=(m // ROW_TILE, nj),
        in_specs=[
            pl.BlockSpec((ROW_TILE, d), lambda i, j: (i, 0)),
            pl.BlockSpec((1, d), lambda i, j: (0, 0)),
            pl.BlockSpec((d, tf), lambda i, j: (0, j)),
            pl.BlockSpec((d, tf), lambda i, j: (0, j + nj)),
            pl.BlockSpec((tf, d), lambda i, j: (j, 0)),
            pl.BlockSpec((1, d), lambda i, j: (0, 0)),
        ],
        out_specs=pl.BlockSpec((ROW_TILE, d), lambda i, j: (i, 0)),
        scratch_shapes=[pltpu.VMEM((ROW_TILE, d), BF), pltpu.VMEM((ROW_TILE, d), F32)],
        compiler_params=_params("parallel", "arbitrary"),
        name="ffn",
    )(x, g_in.reshape(1, d), w_in, w_in, w_out, g_out.reshape(1, d))


def _kv_proj_kernel(x_ref, g_ref, w_ref, cos_ref, sin_ref, o32_ref, o16_ref, xn_ref):
    s = pl.program_id(1)

    @pl.when(s == 0)
    def _():
        xn_ref[...] = _rms(x_ref[...], g_ref[...]).astype(BF)

    y = _dot(xn_ref[...], w_ref[...])

    @pl.when(s % 2 == 0)
    def _():
        cos = cos_ref[...]
        sin = sin_ref[...]
        for h in range(KV_HEADS):
            cols = slice(h * HEAD_DIM, (h + 1) * HEAD_DIM)
            r = _rope(y[:, cols], cos, sin)
            o32_ref[0, :, cols] = r
            o16_ref[0, :, cols] = r.astype(BF)

    @pl.when(s % 2 == 1)
    def _():
        o32_ref[0] = y
        o16_ref[0] = y.astype(BF)


def kv_proj(x, g, w, cos, sin):
    m, d = x.shape
    out = jax.ShapeDtypeStruct((N_KV_SLOTS, m, KV_COLS), F32)
    out16 = jax.ShapeDtypeStruct((N_KV_SLOTS, m, KV_COLS), BF)
    return pl.pallas_call(
        _kv_proj_kernel,
        out_shape=(out, out16),
        grid=(m // ROW_TILE, N_KV_SLOTS),
        in_specs=[
            pl.BlockSpec((ROW_TILE, d), lambda i, s: (i, 0)),
            pl.BlockSpec((1, d), lambda i, s: (0, 0)),
            pl.BlockSpec((d, KV_COLS), lambda i, s: (0, s)),
            pl.BlockSpec((ROW_TILE, HEAD_DIM), lambda i, s: (i, 0)),
            pl.BlockSpec((ROW_TILE, HEAD_DIM), lambda i, s: (i, 0)),
        ],
        out_specs=(pl.BlockSpec((1, ROW_TILE, KV_COLS), lambda i, s: (s, i, 0)),
                   pl.BlockSpec((1, ROW_TILE, KV_COLS), lambda i, s: (s, i, 0))),
        scratch_shapes=[pltpu.VMEM((ROW_TILE, d), BF)],
        compiler_params=_params("parallel", "arbitrary"),
        name="kv_proj",
    )(x, g.reshape(1, d), w, cos, sin)


def _q_proj_kernel(x_ref, g_ref, wq_ref, wg_ref, wgt_ref, cos_ref, sin_ref,
                   q_ref, gate_ref, gatet_ref, xn_ref):
    @pl.when(pl.program_id(1) == 0)
    def _():
        xn = _rms(x_ref[...], g_ref[...]).astype(BF)
        xn_ref[...] = xn
        gate_ref[...] = _sigmoid(_dot(xn, wg_ref[...]))
        gatet_ref[...] = _sigmoid(_dot_nt(wgt_ref[...], xn))

    y = _dot(xn_ref[...], wq_ref[...])
    cos = cos_ref[...]
    sin = sin_ref[...]
    for h in range(GROUP):
        cols = slice(h * HEAD_DIM, (h + 1) * HEAD_DIM)
        q_ref[:, cols] = (_rope(y[:, cols], cos, sin) * SCALE).astype(BF)


def q_proj(x, g, wq, wg_pad, wg_t, cos, sin):
    m, d = x.shape
    n_gate = wg_t.shape[0]
    tn = GROUP * HEAD_DIM
    return pl.pallas_call(
        _q_proj_kernel,
        out_shape=(jax.ShapeDtypeStruct((m, N_HEADS * HEAD_DIM), BF),
                   jax.ShapeDtypeStruct((m, 128), F32),
                   jax.ShapeDtypeStruct((n_gate, m), F32)),
        grid=(m // ROW_TILE, KV_HEADS),
        in_specs=[
            pl.BlockSpec((ROW_TILE, d), lambda i, j: (i, 0)),
            pl.BlockSpec((1, d), lambda i, j: (0, 0)),
            pl.BlockSpec((d, tn), lambda i, j: (0, j)),
            pl.BlockSpec((d, 128), lambda i, j: (0, 0)),
            pl.BlockSpec((n_gate, d), lambda i, j: (0, 0)),
            pl.BlockSpec((ROW_TILE, HEAD_DIM), lambda i, j: (i, 0)),
            pl.BlockSpec((ROW_TILE, HEAD_DIM), lambda i, j: (i, 0)),
        ],
        out_specs=(pl.BlockSpec((ROW_TILE, tn), lambda i, j: (i, j)),
                   pl.BlockSpec((ROW_TILE, 128), lambda i, j: (i, 0)),
                   pl.BlockSpec((n_gate, ROW_TILE), lambda i, j: (0, i))),
        scratch_shapes=[pltpu.VMEM((ROW_TILE, d), BF)],
        compiler_params=_params("parallel", "arbitrary"),
        name="q_proj",
    )(x, g.reshape(1, d), wq, wg_pad, wg_t, cos, sin)


CONV_PAD = 32


def _conv_kernel(buf_ref, u_ref, w_ref, bdw_ref, lg_ref, lb_ref, o_ref, full_ref, y_ref,
                 *, tt, n_t, row_chunk, col_chunk):
    lo = CONV_PAD - CONV_HALO
    if n_t > 1:
        t = pl.program_id(1)

        @pl.when(t == 0)
        def _():
            full_ref[:, lo:CONV_PAD, :] = buf_ref[...]

        @pl.when(t > 0)
        def _():
            full_ref[:, lo:CONV_PAD, :] = full_ref[:, tt + lo:tt + CONV_PAD, :]
    else:
        full_ref[:, lo:CONV_PAD, :] = buf_ref[...]
    full_ref[:, CONV_PAD:CONV_PAD + tt, :] = u_ref[...]

    d = u_ref.shape[-1]
    for r0 in range(0, tt, row_chunk):
        for c0 in range(0, d, col_chunk):
            cols = slice(c0, c0 + col_chunk)
            acc = None
            for k in range(CONV_WIDTH):
                term = w_ref[k:k + 1, cols] * full_ref[:, lo + k + r0:lo + k + r0 + row_chunk, cols]
                acc = term if acc is None else acc + term
            y_ref[:, r0:r0 + row_chunk, cols] = acc + bdw_ref[:, cols]

    y = y_ref[...]
    mu = jnp.mean(y, axis=-1, keepdims=True)
    yc = y - mu
    z = yc * lax.rsqrt(jnp.mean(yc * yc, axis=-1, keepdims=True) + EPS) * lg_ref[...] + lb_ref[...]
    o_ref[...] = z * _sigmoid(z)


def conv_ln(buf, u, wdw, bdw, ln_g, ln_b, *, bb, tt):
    b, t, d = u.shape
    n_t = t // tt
    row_chunk = min(tt, 64)
    kern = functools.partial(_conv_kernel, tt=tt, n_t=n_t, row_chunk=row_chunk, col_chunk=256)
    vec = lambda i, j: (0, 0)
    return pl.pallas_call(
        kern,
        out_shape=jax.ShapeDtypeStruct((b, t, d), F32),
        grid=(b // bb, n_t),
        in_specs=[
            pl.BlockSpec((bb, CONV_HALO, d), lambda i, j: (i, 0, 0)),
            pl.BlockSpec((bb, tt, d), lambda i, j: (i, j, 0)),
            pl.BlockSpec((CONV_WIDTH, d), vec),
            pl.BlockSpec((1, d), vec),
            pl.BlockSpec((1, d), vec),
            pl.BlockSpec((1, d), vec),
        ],
        out_specs=pl.BlockSpec((bb, tt, d), lambda i, j: (i, j, 0)),
        scratch_shapes=[pltpu.VMEM((bb, CONV_PAD + tt, d), F32), pltpu.VMEM((bb, tt, d), F32)],
        compiler_params=_params("parallel", "arbitrary"),
        name="conv_ln",
    )(buf, u, wdw, bdw.reshape(1, d), ln_g.reshape(1, d), ln_b.reshape(1, d))


def _pos_term_kernel(pe_ref, w_ref, o_ref):
    o_ref[0] = _dot(pe_ref[0], w_ref[0])


def pos_term(pe, w1):
    n = pe.shape[0]
    k = CMP_LEN * HEAD_DIM
    pe8 = jnp.broadcast_to(pe.reshape(n, 1, k), (n, 8, k)).astype(BF)
    return pl.pallas_call(
        _pos_term_kernel,
        out_shape=jax.ShapeDtypeStruct((n, 8, CMP_HIDDEN), F32),
        grid=(n,),
        in_specs=[pl.BlockSpec((1, 8, k), lambda i: (i, 0, 0)),
                  pl.BlockSpec((1, k, CMP_HIDDEN), lambda i: (i, 0, 0))],
        out_specs=pl.BlockSpec((1, 8, CMP_HIDDEN), lambda i: (i, 0, 0)),
        compiler_params=_params("parallel"),
        name="pos_term",
    )(pe8, w1.reshape(n, k, CMP_HIDDEN).astype(BF))


def _compress_one(piece_refs, w1_ref, pt_ref, w2_ref, b2_ref, o_ref):
    x = jnp.concatenate([p[0].astype(BF) for p in piece_refs], axis=0)
    n = x.shape[0]
    row = lax.broadcasted_iota(jnp.int32, (n, 1), 0)
    for g in range(KV_HEADS):
        xg = jnp.concatenate(
            [x[:, s * KV_COLS + g * HEAD_DIM: s * KV_COLS + (g + 1) * HEAD_DIM] for s in range(CMP_STRIDE)],
            axis=1)
        fs = _dot(xg, w1_ref[0])
        first = fs[:, :CMP_HIDDEN]
        second = pltpu.roll(fs[:, CMP_HIDDEN:], n - 1, axis=0)
        pre = first + second + pt_ref[0, 0:1, :]
        hid = (pre * _sigmoid(pre)).astype(BF)
        out = _dot(hid, w2_ref[0]) + b2_ref[0]
        out = jnp.where(row < n - 1, out, 0.0)
        o_ref[0, :, g * HEAD_DIM:(g + 1) * HEAD_DIM] = out.astype(o_ref.dtype)


def _compress_kernel(*refs, n_pieces, n_scalar):
    refs = refs[n_scalar:]
    k_pieces = refs[:n_pieces]
    v_pieces = refs[n_pieces:2 * n_pieces]
    w1_ref, pt_ref, w2_ref, b2_ref, ok_ref, ov_ref = refs[2 * n_pieces:]
    _compress_one(k_pieces, w1_ref.at[0:1], pt_ref.at[0:1], w2_ref.at[0:1], b2_ref.at[0:1], ok_ref)
    _compress_one(v_pieces, w1_ref.at[1:2], pt_ref.at[1:2], w2_ref.at[1:2], b2_ref.at[1:2], ov_ref)


def _compress_weight_specs(n_idx):
    fixed3 = lambda *a: (0, 0, 0)
    return [
        pl.BlockSpec((2, CMP_STRIDE * HEAD_DIM, 2 * CMP_HIDDEN), fixed3),
        pl.BlockSpec((2, 8, CMP_HIDDEN), fixed3),
        pl.BlockSpec((2, CMP_HIDDEN, HEAD_DIM), fixed3),
        pl.BlockSpec((2, 1, HEAD_DIM), fixed3),
    ]


def compress_rows(k_rows, v_rows, cw):
    b, n, width = k_rows.shape
    spec = pl.BlockSpec((1, n, width), lambda i: (i, 0, 0))
    out = jax.ShapeDtypeStruct((b, n, KV_COLS), BF)
    ospec = pl.BlockSpec((1, n, KV_COLS), lambda i: (i, 0, 0))
    return pl.pallas_call(
        functools.partial(_compress_kernel, n_pieces=1, n_scalar=0),
        out_shape=(out, out),
        grid=(b,),
        in_specs=[spec, spec] + _compress_weight_specs(1),
        out_specs=(ospec, ospec),
        compiler_params=_params("parallel"),
        name="compress_rows",
    )(k_rows, v_rows, *cw)


def compress_pages(page_table, n_pages, cache_k, cache_v, cw):
    n_pool, cpp, width = cache_k.shape
    b = page_table.shape[0] // n_pages
    n = n_pages * cpp

    def page_spec(p):
        return pl.BlockSpec((1, cpp, width), lambda i, pt: (pt[i * n_pages + p], 0, 0))

    pages = [page_spec(p) for p in range(n_pages)]
    out = jax.ShapeDtypeStruct((b, n, KV_COLS), BF)
    ospec = pl.BlockSpec((1, n, KV_COLS), lambda i, pt: (i, 0, 0))
    grid_spec = pltpu.PrefetchScalarGridSpec(
        num_scalar_prefetch=1,
        grid=(b,),
        in_specs=pages + pages + _compress_weight_specs(2),
        out_specs=(ospec, ospec),
    )
    return pl.pallas_call(
        functools.partial(_compress_kernel, n_pieces=n_pages, n_scalar=1),
        out_shape=(out, out),
        grid_spec=grid_spec,
        compiler_params=_params("parallel"),
        name="compress_pages",
    )(page_table, *([cache_k] * n_pages), *([cache_v] * n_pages), *cw)


def _softmax_cols(s, mask):
    s = jnp.where(mask, s, NEG_INF)
    m = jnp.max(s, axis=0, keepdims=True)
    m = jnp.where(m == NEG_INF, 0.0, m)
    p = jnp.where(mask, jnp.exp(s - m), 0.0)
    return p / jnp.maximum(jnp.sum(p, axis=0, keepdims=True), 1e-30)


def _top_rank_mask(score, valid, axis, n):
    idx = lax.broadcasted_iota(jnp.int32, score.shape, axis)
    cnt = jnp.zeros(score.shape, F32)
    for j in range(n):
        other = score[j:j + 1, :] if axis == 0 else score[:, j:j + 1]
        beats = (other > score) | ((other == score) & (idx > j))
        cnt = cnt + jnp.where(beats, 1.0, 0.0)
    return jnp.where((cnt < TOP_N) & valid, 1.0, 0.0)


def _flash_step(q_rows, k, vt, bias, carry):
    m, l, acc = carry
    s = _dot_nt(k, q_rows) + bias
    m_new = jnp.maximum(m, jnp.max(s, axis=0, keepdims=True))
    m_safe = jnp.where(m_new == NEG_INF, 0.0, m_new)
    alpha = jnp.exp(m - m_safe)
    p = jnp.exp(s - m_safe)
    l = alpha * l + jnp.sum(p, axis=0, keepdims=True)
    acc = alpha * acc + _dot(vt, p.astype(BF))
    return m_new, l, acc


def _prompt_attn_kernel(q_ref, gt_ref, kc_ref, vct_ref, ks_ref, vst_ref, kw_ref, vwt_ref,
                        o_ref, bias_ref, *, n_cmp):
    qt = pl.program_id(2)
    n_rows = GROUP * Q_TILE
    q = q_ref[0]
    q_rows = jnp.concatenate([q[:, r * HEAD_DIM:(r + 1) * HEAD_DIM] for r in range(GROUP)], axis=0)
    t0 = qt * Q_TILE
    pos_t = t0 + lax.broadcasted_iota(jnp.int32, (1, Q_TILE), 1)
    pos_rows = jnp.concatenate([pos_t] * GROUP, axis=1)

    n_cpad = kc_ref.shape[1]
    ci = lax.broadcasted_iota(jnp.int32, (n_cpad, 1), 0)
    cmask = (ci * CMP_STRIDE + (CMP_LEN - 1) <= pos_rows) & (ci < n_cmp)
    p_cmp = _softmax_cols(_dot_nt(kc_ref[0], q_rows), cmask)
    o_cmp = _dot(vct_ref[0, 0], p_cmp.astype(BF))

    n_sel = ks_ref.shape[2] * KEY_TILE // SEL_BLOCK
    p_sum = p_cmp[:, 0:Q_TILE]
    for r in range(1, GROUP):
        p_sum = p_sum + p_cmp[:, r * Q_TILE:(r + 1) * Q_TILE]
    bj = lax.broadcasted_iota(jnp.int32, (n_sel, n_cpad), 0) * SEL_BLOCK
    bi = lax.broadcasted_iota(jnp.int32, (n_sel, n_cpad), 1) * CMP_STRIDE
    overlap_t = jnp.where((bi < bj + SEL_BLOCK) & (bi + CMP_LEN > bj), 1.0, 0.0).astype(BF)
    hi, mid, lo = _split3(p_sum)
    imp = _dot(overlap_t, hi) + _dot(overlap_t, mid) + _dot(overlap_t, lo)
    blk = lax.broadcasted_iota(jnp.int32, (n_sel, 1), 0)
    cur = pos_t >> SEL_SHIFT
    forced = (blk == 0) | (blk == cur) | (blk == cur - 1)
    causal = blk * SEL_BLOCK <= pos_t
    score = jnp.where(causal, imp + FORCE_BONUS * jnp.where(forced, 1.0, 0.0), NEG_INF)
    sel = _top_rank_mask(score, causal, 0, n_sel)

    key_in_tile = lax.broadcasted_iota(jnp.int32, (KEY_TILE, 1), 0)
    per_tile = KEY_TILE // SEL_BLOCK
    for kt in range(ks_ref.shape[2]):
        picked = jnp.zeros((KEY_TILE, Q_TILE), F32)
        for h in range(per_tile):
            in_blk = (key_in_tile >= h * SEL_BLOCK) & (key_in_tile < (h + 1) * SEL_BLOCK)
            row = sel[kt * per_tile + h:kt * per_tile + h + 1, :]
            picked = jnp.where(in_blk, row, picked)
        ok = (picked > 0.5) & (kt * KEY_TILE + key_in_tile <= pos_t)
        bias_ref[kt] = jnp.where(ok, 0.0, NEG_INF)

    init = (jnp.full((1, n_rows), NEG_INF, F32), jnp.zeros((1, n_rows), F32),
            jnp.zeros((HEAD_DIM, n_rows), F32))

    def sel_step(kt, carry):
        b = bias_ref[kt]
        bias = jnp.concatenate([b] * GROUP, axis=1)
        return _flash_step(q_rows, ks_ref[0, 0, kt], vst_ref[0, 0, kt], bias, carry)

    _, l_sel, acc_sel = lax.fori_loop(0, qt + 1, sel_step, init)
    o_sel = acc_sel / jnp.maximum(l_sel, 1e-30)

    def win_step(kt, carry):
        kpos = kt * KEY_TILE + key_in_tile
        ok = (kpos <= pos_rows) & (pos_rows - kpos < WINDOW)
        bias = jnp.where(ok, 0.0, NEG_INF)
        return _flash_step(q_rows, kw_ref[0, 0, kt], vwt_ref[0, 0, kt], bias, carry)

    first = jnp.maximum(qt - WINDOW // KEY_TILE, 0)
    _, l_win, acc_win = lax.fori_loop(first, qt + 1, win_step, init)
    o_win = acc_win / jnp.maximum(l_win, 1e-30)

    gt = gt_ref[0]
    for r in range(GROUP):
        cols = slice(r * Q_TILE, (r + 1) * Q_TILE)
        base = r * N_BRANCH
        mix = (gt[base:base + 1, :] * o_cmp[:, cols] + gt[base + 1:base + 2, :] * o_sel[:, cols]
               + gt[base + 2:base + 3, :] * o_win[:, cols])
        o_ref[0, :, r * HEAD_DIM:(r + 1) * HEAD_DIM] = mix.T.astype(o_ref.dtype)


def prompt_attention(q, gates_t, kc, vct, ks, vst, kw, vwt, n_cmp):
    b, t, _ = q.shape
    n_q = t // Q_TILE
    n_kt = ks.shape[2]
    n_cpad = kc.shape[1]
    tile_spec = pl.BlockSpec((1, 1, n_kt, KEY_TILE, HEAD_DIM), lambda i, g, j: (i, g, 0, 0, 0))
    width = GROUP * HEAD_DIM
    return pl.pallas_call(
        functools.partial(_prompt_attn_kernel, n_cmp=n_cmp),
        out_shape=jax.ShapeDtypeStruct((b, t, N_HEADS * HEAD_DIM), BF),
        grid=(b, KV_HEADS, n_q),
        in_specs=[
            pl.BlockSpec((1, Q_TILE, width), lambda i, g, j: (i, j, g)),
            pl.BlockSpec((1, GROUP * N_BRANCH, Q_TILE), lambda i, g, j: (g, 0, i * n_q + j)),
            pl.BlockSpec((1, n_cpad, HEAD_DIM), lambda i, g, j: (i, 0, g)),
            pl.BlockSpec((1, 1, HEAD_DIM, n_cpad), lambda i, g, j: (i, g, 0, 0)),
            tile_spec, tile_spec, tile_spec, tile_spec,
        ],
        out_specs=pl.BlockSpec((1, Q_TILE, width), lambda i, g, j: (i, j, g)),
        scratch_shapes=[pltpu.VMEM((n_kt, KEY_TILE, Q_TILE), F32)],
        compiler_params=_params("parallel", "parallel", "arbitrary"),
        name="prompt_attention",
    )(q, gates_t, kc, vct, ks, vst, kw, vwt)


def _softmax_two(s_a, ok_a, s_b, ok_b):
    s_a = jnp.where(ok_a, s_a, NEG_INF)
    s_b = jnp.where(ok_b, s_b, NEG_INF)
    m = jnp.maximum(jnp.max(s_a, axis=1, keepdims=True), jnp.max(s_b, axis=1, keepdims=True))
    m = jnp.where(m == NEG_INF, 0.0, m)
    p_a = jnp.where(ok_a, jnp.exp(s_a - m), 0.0)
    p_b = jnp.where(ok_b, jnp.exp(s_b - m), 0.0)
    total = jnp.sum(p_a, axis=1, keepdims=True) + jnp.sum(p_b, axis=1, keepdims=True)
    return p_a, p_b, 1.0 / jnp.maximum(total, 1e-30)


def _head_diag(o):
    rows = o.shape[0] // KV_HEADS
    return jnp.concatenate(
        [o[g * rows:(g + 1) * rows, g * HEAD_DIM:(g + 1) * HEAD_DIM] for g in range(KV_HEADS)], axis=0)


def _sample_attn_kernel(*refs, n_pages, t_new, past_len, n_cmp):
    refs = refs[1:]
    q_ref, gate_ref, kc_ref, vc_ref = refs[:4]
    ks_pages = refs[4:4 + n_pages]
    vs_pages = refs[4 + n_pages:4 + 2 * n_pages]
    ksn_ref, vsn_ref, kwp_ref, vwp_ref, kwn_ref, vwn_ref, o_ref = refs[4 + 2 * n_pages:]

    n_rows = q_ref.shape[1]
    rows_per_head = n_rows // KV_HEADS
    q = q_ref[0]
    row = lax.broadcasted_iota(jnp.int32, (n_rows, 1), 0)
    lane = lax.broadcasted_iota(jnp.int32, (1, KV_COLS), 1)
    q_tiled = jnp.concatenate([q] * KV_HEADS, axis=1)
    same_head = _div(row, rows_per_head) == _div(lane, HEAD_DIM)
    q_bd = jnp.where(same_head, q_tiled, jnp.zeros_like(q_tiled))
    pos = past_len + _mod(row, t_new)
    new_lane = lax.broadcasted_iota(jnp.int32, (1, 128), 1)
    new_pos = past_len + new_lane
    new_ok = (new_lane < t_new) & (new_pos <= pos)

    def pad_new(ref):
        x = ref[0]
        return jnp.concatenate([x, jnp.zeros((128 - x.shape[0], x.shape[1]), x.dtype)], axis=0).astype(BF)

    n_cpad = kc_ref.shape[1]
    ci = lax.broadcasted_iota(jnp.int32, (1, n_cpad), 1)
    cmask = (ci * CMP_STRIDE + (CMP_LEN - 1) <= pos) & (ci < n_cmp)
    s_c = jnp.where(cmask, _dot_nt(q_bd, kc_ref[0]), NEG_INF)
    m_c = jnp.max(s_c, axis=1, keepdims=True)
    m_c = jnp.where(m_c == NEG_INF, 0.0, m_c)
    p_c = jnp.where(cmask, jnp.exp(s_c - m_c), 0.0)
    p_c = p_c / jnp.maximum(jnp.sum(p_c, axis=1, keepdims=True), 1e-30)
    o_cmp = _head_diag(_dot(p_c.astype(BF), vc_ref[0]))

    ra = lax.broadcasted_iota(jnp.int32, (n_rows, n_rows), 0)
    rb = lax.broadcasted_iota(jnp.int32, (n_rows, n_rows), 1)
    same_gt = ((_div(ra, rows_per_head) == _div(rb, rows_per_head))
               & (_mod(ra, t_new) == _mod(rb, t_new)))
    group_sum = jnp.where(same_gt, 1.0, 0.0).astype(BF)
    hi, mid, lo = _split3(p_c)
    p_sum = _dot(group_sum, hi) + _dot(group_sum, mid) + _dot(group_sum, lo)
    n_keys = past_len + t_new
    n_sel = -(-n_keys // SEL_BLOCK)
    bi = lax.broadcasted_iota(jnp.int32, (n_cpad, 128), 0) * CMP_STRIDE
    bj = lax.broadcasted_iota(jnp.int32, (n_cpad, 128), 1) * SEL_BLOCK
    overlap = jnp.where((bi < bj + SEL_BLOCK) & (bi + CMP_LEN > bj), 1.0, 0.0).astype(BF)
    imp = _dot01(p_sum, overlap)
    blk = lax.broadcasted_iota(jnp.int32, (1, 128), 1)
    cur = pos >> SEL_SHIFT
    forced = (blk == 0) | (blk == cur) | (blk == cur - 1)
    causal = (blk * SEL_BLOCK <= pos) & (blk < n_sel)
    score = jnp.where(causal, imp + FORCE_BONUS * jnp.where(forced, 1.0, 0.0), NEG_INF)
    sel = _top_rank_mask(score, causal, 1, n_sel)

    k_past = jnp.concatenate([p[0].astype(BF) for p in ks_pages], axis=0)
    v_past = jnp.concatenate([p[0].astype(BF) for p in vs_pages], axis=0)
    ej = lax.broadcasted_iota(jnp.int32, (128, past_len), 0)
    ek = lax.broadcasted_iota(jnp.int32, (128, past_len), 1)
    expand = jnp.where((ek >> SEL_SHIFT) == ej, 1.0, 0.0).astype(BF)
    key = lax.broadcasted_iota(jnp.int32, (1, past_len), 1)
    ok_past = (_dot(sel.astype(BF), expand) > 0.5) & (key <= pos)
    nj = lax.broadcasted_iota(jnp.int32, (128, 128), 0)
    nk = past_len + lax.broadcasted_iota(jnp.int32, (128, 128), 1)
    expand_new = jnp.where((nk >> SEL_SHIFT) == nj, 1.0, 0.0).astype(BF)
    ok_new = new_ok & (_dot(sel.astype(BF), expand_new) > 0.5)
    p_a, p_b, inv = _softmax_two(_dot_nt(q_bd, k_past), ok_past, _dot_nt(q_bd, pad_new(ksn_ref)), ok_new)
    o_sel = _head_diag((_dot(p_a.astype(BF), v_past) + _dot(p_b.astype(BF), pad_new(vsn_ref))) * inv)

    n_win = kwp_ref.shape[1]
    wpos = past_len - n_win + lax.broadcasted_iota(jnp.int32, (1, n_win), 1)
    ok_wp = (wpos <= pos) & (pos - wpos < WINDOW)
    ok_wn = new_ok & (pos - new_pos < WINDOW)
    p_a, p_b, inv = _softmax_two(_dot_nt(q_bd, kwp_ref[0].astype(BF)), ok_wp,
                                 _dot_nt(q_bd, pad_new(kwn_ref)), ok_wn)
    o_win = _head_diag((_dot(p_a.astype(BF), vwp_ref[0].astype(BF))
                        + _dot(p_b.astype(BF), pad_new(vwn_ref))) * inv)

    gate = gate_ref[0]
    o = gate[:, 0:1] * o_cmp + gate[:, 1:2] * o_sel + gate[:, 2:3] * o_win
    o_ref[0] = o.astype(o_ref.dtype)


def sample_attention(page_table, q, gates, kc, vc, cache_ks, cache_vs, ks_new, vs_new,
                     kw_past, vw_past, kw_new, vw_new, *, t_new, n_cmp):
    b, n_rows, _ = q.shape
    n_pages = page_table.shape[0] // b
    page = cache_ks.shape[1]
    n_cpad = kc.shape[1]
    n_win = kw_past.shape[1]
    n_new = ks_new.shape[1]

    def page_spec(p):
        return pl.BlockSpec((1, page, KV_COLS), lambda i, pt: (pt[i * n_pages + p], 0, 0))

    per_seq = lambda shape: pl.BlockSpec((1,) + shape, lambda i, pt: (i, 0, 0))
    pages = [page_spec(p) for p in range(n_pages)]
    grid_spec = pltpu.PrefetchScalarGridSpec(
        num_scalar_prefetch=1,
        grid=(b,),
        in_specs=[per_seq((n_rows, HEAD_DIM)), per_seq((n_rows, N_BRANCH)),
                  per_seq((n_cpad, KV_COLS)), per_seq((n_cpad, KV_COLS))]
        + pages + pages
        + [per_seq((n_new, KV_COLS)), per_seq((n_new, KV_COLS)),
           per_seq((n_win, KV_COLS)), per_seq((n_win, KV_COLS)),
           per_seq((n_new, KV_COLS)), per_seq((n_new, KV_COLS))],
        out_specs=per_seq((n_rows, HEAD_DIM)),
    )
    kern = functools.partial(_sample_attn_kernel, n_pages=n_pages, t_new=t_new,
                             past_len=n_pages * page, n_cmp=n_cmp)
    return pl.pallas_call(
        kern,
        out_shape=jax.ShapeDtypeStruct((b, n_rows, HEAD_DIM), BF),
        grid_spec=grid_spec,
        compiler_params=_params("parallel"),
        name="sample_attention",
    )(page_table, q, gates, kc, vc, *([cache_ks] * n_pages), *([cache_vs] * n_pages),
      ks_new, vs_new, kw_past, vw_past, kw_new, vw_new)


def _rope_tables(pos):
    half = HEAD_DIM // 2
    inv = ROPE_THETA ** (-jnp.arange(half, dtype=F32) / half)
    ang = pos.astype(F32)[:, None] * inv[None, :]
    cos = jnp.cos(ang)
    sin = jnp.sin(ang)
    return jnp.concatenate([cos, cos], axis=1), jnp.concatenate([-sin, sin], axis=1)


def kernel(x_prompt, x_sample, cache_k_cmp, cache_v_cmp, cache_k_sel, cache_v_sel, state_k_win, state_v_win, state_conv, page_table, norm_g, conv_w1, conv_b1, conv_wdw, conv_bdw, conv_ln_g, conv_ln_b, conv_w2, conv_b2, kv_norm_g, w_kv, cmp_w1, cmp_pe, cmp_w2, cmp_b2, nsa_w_qg, nsa_w_o, ffn_w_in, ffn_w_out):
    bp, tp, d = x_prompt.shape
    bs, ts, _ = x_sample.shape
    n_p = bp * tp
    n_s = bs * ts
    n_pool, page, _, _ = cache_k_sel.shape
    n_pages = page_table.shape[1]
    past_len = n_pages * page
    bf = lambda w: w.astype(BF)

    def split(a):
        return a[:n_p], a[n_p:]

    x = jnp.concatenate([x_prompt.reshape(n_p, d), x_sample.reshape(n_s, d)], axis=0)
    pos = jnp.concatenate([jnp.tile(jnp.arange(tp, dtype=jnp.int32), bp),
                           jnp.tile(past_len + jnp.arange(ts, dtype=jnp.int32), bs)])
    cos, sin = _rope_tables(pos)

    g = norm_g[0]
    u = glu_in(x, g[0], bf(conv_w1[0]), conv_b1[0])
    u_p, u_s = split(u)
    u_p = u_p.reshape(bp, tp, d)
    u_s = u_s.reshape(bs, ts, d)
    conv_args = (conv_wdw[0], conv_bdw[0], conv_ln_g[0], conv_ln_b[0])
    v_p = conv_ln(jnp.zeros((bp, CONV_HALO, d), F32), u_p, *conv_args, bb=1, tt=128)
    v_s = conv_ln(state_conv[0], u_s, *conv_args, bb=16, tt=ts)
    v = jnp.concatenate([v_p.reshape(n_p, d), v_s.reshape(n_s, d)], axis=0)
    x = mm_norm_res(v, bf(conv_w2[0]), conv_b2[0], g[1], x)
    x = ffn(x, g[2], bf(ffn_w_in[0]), bf(ffn_w_out[0]), g[3])
    conv_p = u_p[:, tp - CONV_HALO:][None]
    conv_s = jnp.concatenate([state_conv[0][:, ts:], u_s], axis=1)[None]

    kv32, kv16 = kv_proj(x, kv_norm_g, bf(w_kv), cos, sin)
    rows_p = [kv32[s, :n_p].reshape(bp, tp, KV_HEADS, HEAD_DIM) for s in range(N_KV_SLOTS)]
    rows_s = [kv32[s, n_p:].reshape(bs, ts, KV_HEADS, HEAD_DIM) for s in range(N_KV_SLOTS)]
    n_wp = min(WINDOW, tp)
    kw_p, vw_p = rows_p[4][:, tp - n_wp:], rows_p[5][:, tp - n_wp:]
    kw_s = jnp.concatenate([state_k_win[:, ts:], rows_s[4]], axis=1)
    vw_s = jnp.concatenate([state_v_win[:, ts:], rows_s[5]], axis=1)

    w1 = cmp_w1.reshape(2, 2, CMP_STRIDE * HEAD_DIM, CMP_HIDDEN)
    w1cat = bf(jnp.concatenate([w1[:, 0], w1[:, 1]], axis=-1))
    cw = (w1cat, pos_term(cmp_pe, cmp_w1), bf(cmp_w2), cmp_b2.reshape(2, 1, HEAD_DIM))
    chunk_cols = CMP_STRIDE * KV_COLS
    kc_p, vc_p = compress_rows(kv16[0, :n_p].reshape(bp, tp // CMP_STRIDE, chunk_cols),
                               kv16[1, :n_p].reshape(bp, tp // CMP_STRIDE, chunk_cols), cw)
    pt_flat = page_table.reshape(-1)
    kc_s, vc_s = compress_pages(pt_flat, n_pages,
                                cache_k_cmp.reshape(n_pool, page // CMP_STRIDE, chunk_cols),
                                cache_v_cmp.reshape(n_pool, page // CMP_STRIDE, chunk_cols), cw)

    g = norm_g[1]
    w_qg = nsa_w_qg[0]
    n_q_cols = N_HEADS * HEAD_DIM
    n_gate = N_HEADS * N_BRANCH
    wg = w_qg[:, n_q_cols:]
    wg_pad = bf(jnp.pad(wg, ((0, 0), (0, 128 - n_gate))))
    q, gates, gates_t = q_proj(x, g[0], bf(w_qg[:, :n_q_cols]), wg_pad, bf(wg.T), cos, sin)

    n_kt = tp // KEY_TILE

    def key_tiles(a):
        return a.reshape(bp, n_kt, KEY_TILE, KV_HEADS, HEAD_DIM).transpose(0, 3, 1, 2, 4)

    def value_tiles_t(a):
        return a.reshape(bp, n_kt, KEY_TILE, KV_HEADS, HEAD_DIM).transpose(0, 3, 1, 4, 2)

    n_cmp_p = tp // CMP_STRIDE - 1
    vct_p = vc_p.reshape(bp, tp // CMP_STRIDE, KV_HEADS, HEAD_DIM).transpose(0, 2, 3, 1)
    gates_t_p = gates_t[:, :n_p].reshape(KV_HEADS, GROUP * N_BRANCH, n_p)
    o_p = prompt_attention(q[:n_p].reshape(bp, tp, n_q_cols), gates_t_p, kc_p, vct_p,
                           key_tiles(kv16[2, :n_p]), value_tiles_t(kv16[3, :n_p]),
                           key_tiles(kv16[4, :n_p]), value_tiles_t(kv16[5, :n_p]), n_cmp_p)

    def seq_rows(a, width):
        return a.reshape(bs, ts, KV_HEADS, GROUP, width).transpose(0, 2, 3, 1, 4).reshape(bs, -1, width)

    q_s = seq_rows(q[n_p:], HEAD_DIM)
    gates_s = seq_rows(gates[n_p:, :n_gate], N_BRANCH)

    def new_rows(slot):
        return jnp.pad(kv32[slot, n_p:].reshape(bs, ts, KV_COLS), ((0, 0), (0, 8 - ts), (0, 0)))

    o_s = sample_attention(
        pt_flat, q_s, gates_s, kc_s, vc_s,
        cache_k_sel.reshape(n_pool, page, KV_COLS), cache_v_sel.reshape(n_pool, page, KV_COLS),
        new_rows(2), new_rows(3),
        state_k_win.reshape(bs, -1, KV_COLS), state_v_win.reshape(bs, -1, KV_COLS),
        new_rows(4), new_rows(5), t_new=ts, n_cmp=past_len // CMP_STRIDE - 1)
    o_s = o_s.reshape(bs, KV_HEADS, GROUP, ts, HEAD_DIM).transpose(0, 3, 1, 2, 4).reshape(n_s, n_q_cols)
    o = jnp.concatenate([o_p.reshape(n_p, n_q_cols), o_s], axis=0)
    x = mm_norm_res(o, bf(nsa_w_o[0]), None, g[1], x)
    x = ffn(x, g[2], bf(ffn_w_in[1]), bf(ffn_w_out[1]), g[3])

    y_p, y_s = split(x)
    return (y_p.reshape(bp, tp, d), y_s.reshape(bs, ts, d), conv_p,
            rows_p[0], rows_p[1], rows_p[2], rows_p[3], kw_p, vw_p,
            conv_s, rows_s[0], rows_s[1], rows_s[2], rows_s[3], kw_s, vw_s)
```

```python
import functools

import jax
import jax.numpy as jnp
from jax import lax
from jax.experimental import pallas as pl
from jax.experimental.pallas import tpu as pltpu

D_MODEL = 2048
CONV_WIDTH = 31
CONV_HALO = CONV_WIDTH - 1
N_HEADS = 16
HEAD_DIM = 128
KV_HEADS = 4
GROUP = N_HEADS // KV_HEADS
N_BRANCH = 3
N_KV_SLOTS = 6
KV_COLS = KV_HEADS * HEAD_DIM
CMP_LEN = 32
CMP_STRIDE = 16
CMP_HIDDEN = 256
SEL_BLOCK = 64
TOP_N = 16
WINDOW = 512
ROPE_THETA = 10000.0
FORCE_BONUS = 1.0e4
EPS = 1e-6
SCALE = HEAD_DIM ** -0.5
SEL_SHIFT = SEL_BLOCK.bit_length() - 1
assert 1 << SEL_SHIFT == SEL_BLOCK

BF = jnp.bfloat16
F32 = jnp.float32
NEG_INF = float("-inf")

VMEM_LIMIT_BYTES = 56 * 1024 * 1024
ROW_TILE = 512
KEY_TILE = 128
KEY_STEP = 512
Q_TILE = 128


def _params(*sem):
    return pltpu.CompilerParams(dimension_semantics=sem, vmem_limit_bytes=VMEM_LIMIT_BYTES)


def _sigmoid(x):
    return 1.0 / (1.0 + jnp.exp(-x))


def _rms(x, g):
    return x * lax.rsqrt(jnp.mean(x * x, axis=-1, keepdims=True) + EPS) * g


def _rope(y, cos, sin_signed):
    return y * cos + pltpu.roll(y, HEAD_DIM // 2, axis=1) * sin_signed


def _split3(x):
    hi = x.astype(BF)
    r1 = x - hi.astype(F32)
    mid = r1.astype(BF)
    lo = (r1 - mid.astype(F32)).astype(BF)
    return hi, mid, lo


def _div(x, n):
    assert n & (n - 1) == 0
    return x >> (n.bit_length() - 1)


def _mod(x, n):
    assert n & (n - 1) == 0
    return x & (n - 1)


def _dot(a, b):
    return jnp.dot(a, b, preferred_element_type=F32)


def _dot_nt(a, b):
    return lax.dot_general(a, b, (((1,), (1,)), ((), ())), preferred_element_type=F32)


def _dot01(x, onehot):
    hi, mid, lo = _split3(x)
    return _dot(hi, onehot) + _dot(mid, onehot) + _dot(lo, onehot)


def _glu_in_kernel(x_ref, g_ref, wa_ref, wb_ref, ba_ref, bb_ref, o_ref, xn_ref):
    @pl.when(pl.program_id(1) == 0)
    def _():
        xn_ref[...] = _rms(x_ref[...], g_ref[...]).astype(BF)

    xn = xn_ref[...]
    a = _dot(xn, wa_ref[...]) + ba_ref[...]
    b = _dot(xn, wb_ref[...]) + bb_ref[...]
    o_ref[...] = a * _sigmoid(b)


def glu_in(x, g, w, b):
    m, d = x.shape
    tn = 512
    nj = d // tn
    return pl.pallas_call(
        _glu_in_kernel,
        out_shape=jax.ShapeDtypeStruct((m, d), F32),
        grid=(m // ROW_TILE, nj),
        in_specs=[
            pl.BlockSpec((ROW_TILE, d), lambda i, j: (i, 0)),
            pl.BlockSpec((1, d), lambda i, j: (0, 0)),
            pl.BlockSpec((d, tn), lambda i, j: (0, j)),
            pl.BlockSpec((d, tn), lambda i, j: (0, j + nj)),
            pl.BlockSpec((1, tn), lambda i, j: (0, j)),
            pl.BlockSpec((1, tn), lambda i, j: (0, j + nj)),
        ],
        out_specs=pl.BlockSpec((ROW_TILE, tn), lambda i, j: (i, j)),
        scratch_shapes=[pltpu.VMEM((ROW_TILE, d), BF)],
        compiler_params=_params("parallel", "arbitrary"),
        name="glu_in",
    )(x, g.reshape(1, d), w, w, b.reshape(1, 2 * d), b.reshape(1, 2 * d))


def _mm_norm_res_kernel(*refs, has_bias):
    if has_bias:
        h_ref, w_ref, b_ref, g_ref, x_ref, o_ref = refs
    else:
        h_ref, w_ref, g_ref, x_ref, o_ref = refs
    y = _dot(h_ref[...].astype(BF), w_ref[...])
    if has_bias:
        y = y + b_ref[...]
    o_ref[...] = x_ref[...] + _rms(y, g_ref[...])


def mm_norm_res(h, w, b, g, x):
    m, k = h.shape
    d = w.shape[1]
    tm = 256
    row = lambda i: (i, 0)
    fixed = lambda i: (0, 0)
    in_specs = [pl.BlockSpec((tm, k), row), pl.BlockSpec((k, d), fixed)]
    args = [h, w]
    if b is not None:
        in_specs.append(pl.BlockSpec((1, d), fixed))
        args.append(b.reshape(1, d))
    in_specs += [pl.BlockSpec((1, d), fixed), pl.BlockSpec((tm, d), row)]
    args += [g.reshape(1, d), x]
    return pl.pallas_call(
        functools.partial(_mm_norm_res_kernel, has_bias=b is not None),
        out_shape=jax.ShapeDtypeStruct((m, d), F32),
        grid=(m // tm,),
        in_specs=in_specs,
        out_specs=pl.BlockSpec((tm, d), row),
        compiler_params=_params("parallel"),
        name="mm_norm_res",
    )(*args)


def _ffn_kernel(x_ref, gi_ref, wa_ref, wb_ref, wo_ref, go_ref, o_ref, xn_ref, acc_ref):
    j = pl.program_id(1)

    @pl.when(j == 0)
    def _():
        xn_ref[...] = _rms(x_ref[...], gi_ref[...]).astype(BF)
        acc_ref[...] = jnp.zeros_like(acc_ref)

    xn = xn_ref[...]
    a = _dot(xn, wa_ref[0])
    b = _dot(xn, wb_ref[0])
    h = (a * _sigmoid(a) * b).astype(BF)
    acc_ref[...] += _dot(h, wo_ref[0])

    @pl.when(j == pl.num_programs(1) - 1)
    def _():
        o_ref[...] = x_ref[...] + _rms(acc_ref[...], go_ref[...])


def ffn(x, g_in, w_in, w_out, g_out, layer):
    m, d = x.shape
    f = w_out.shape[1]
    tf = 512
    nj = f // tf
    return pl.pallas_call(
        _ffn_kernel,
        out_shape=jax.ShapeDtypeStruct((m, d), F32),
        grid=(m // ROW_TILE, nj),
        in_specs=[
            pl.BlockSpec((ROW_TILE, d), lambda i, j: (i, 0)),
            pl.BlockSpec((1, d), lambda i, j: (0, 0)),
            pl.BlockSpec((1, d, tf), lambda i, j: (layer, 0, j)),
            pl.BlockSpec((1, d, tf), lambda i, j: (layer, 0, j + nj)),
            pl.BlockSpec((1, tf, d), lambda i, j: (layer, j, 0)),
            pl.BlockSpec((1, d), lambda i, j: (0, 0)),
        ],
        out_specs=pl.BlockSpec((ROW_TILE, d), lambda i, j: (i, 0)),
        scratch_shapes=[pltpu.VMEM((ROW_TILE, d), BF), pltpu.VMEM((ROW_TILE, d), F32)],
        compiler_params=_params("parallel", "arbitrary"),
        name="ffn",
    )(x, g_in.reshape(1, d), w_in, w_in, w_out, g_out.reshape(1, d))


def _kv_proj_kernel(x_ref, g_ref, w_ref, cos_ref, sin_ref, o32_ref, o16_ref, xn_ref):
    s = pl.program_id(1)

    @pl.when(s == 0)
    def _():
        xn_ref[...] = _rms(x_ref[...], g_ref[...]).astype(BF)

    y = _dot(xn_ref[...], w_ref[...])

    @pl.when(s % 2 == 0)
    def _():
        cos = cos_ref[...]
        sin = sin_ref[...]
        for h in range(KV_HEADS):
            cols = slice(h * HEAD_DIM, (h + 1) * HEAD_DIM)
            r = _rope(y[:, cols], cos, sin)
            o32_ref[0, :, cols] = r
            o16_ref[0, :, cols] = r.astype(BF)

    @pl.when(s % 2 == 1)
    def _():
        o32_ref[0] = y
        o16_ref[0] = y.astype(BF)


def kv_proj(x, g, w, cos, sin):
    m, d = x.shape
    out = jax.ShapeDtypeStruct((N_KV_SLOTS, m, KV_COLS), F32)
    out16 = jax.ShapeDtypeStruct((N_KV_SLOTS, m, KV_COLS), BF)
    return pl.pallas_call(
        _kv_proj_kernel,
        out_shape=(out, out16),
        grid=(m // ROW_TILE, N_KV_SLOTS),
        in_specs=[
            pl.BlockSpec((ROW_TILE, d), lambda i, s: (i, 0)),
            pl.BlockSpec((1, d), lambda i, s: (0, 0)),
            pl.BlockSpec((d, KV_COLS), lambda i, s: (0, s)),
            pl.BlockSpec((ROW_TILE, HEAD_DIM), lambda i, s: (i, 0)),
            pl.BlockSpec((ROW_TILE, HEAD_DIM), lambda i, s: (i, 0)),
        ],
        out_specs=(pl.BlockSpec((1, ROW_TILE, KV_COLS), lambda i, s: (s, i, 0)),
                   pl.BlockSpec((1, ROW_TILE, KV_COLS), lambda i, s: (s, i, 0))),
        scratch_shapes=[pltpu.VMEM((ROW_TILE, d), BF)],
        compiler_params=_params("parallel", "arbitrary"),
        name="kv_proj",
    )(x, g.reshape(1, d), w, cos, sin)


def _q_proj_kernel(x_ref, g_ref, wq_ref, wg_ref, wgt_ref, cos_ref, sin_ref,
                   q_ref, gate_ref, gatet_ref, xn_ref):
    @pl.when(pl.program_id(1) == 0)
    def _():
        xn = _rms(x_ref[...], g_ref[...]).astype(BF)
        xn_ref[...] = xn
        gate_ref[...] = _sigmoid(_dot(xn, wg_ref[...]))
        gatet_ref[...] = _sigmoid(_dot_nt(wgt_ref[...], xn))

    y = _dot(xn_ref[...], wq_ref[...])
    cos = cos_ref[...]
    sin = sin_ref[...]
    for h in range(GROUP):
        cols = slice(h * HEAD_DIM, (h + 1) * HEAD_DIM)
        q_ref[:, cols] = (_rope(y[:, cols], cos, sin) * SCALE).astype(BF)


def q_proj(x, g, wq, wg_pad, wg_t, cos, sin):
    m, d = x.shape
    n_gate = wg_t.shape[0]
    tn = GROUP * HEAD_DIM
    return pl.pallas_call(
        _q_proj_kernel,
        out_shape=(jax.ShapeDtypeStruct((m, N_HEADS * HEAD_DIM), BF),
                   jax.ShapeDtypeStruct((m, 128), F32),
                   jax.ShapeDtypeStruct((n_gate, m), F32)),
        grid=(m // ROW_TILE, KV_HEADS),
        in_specs=[
            pl.BlockSpec((ROW_TILE, d), lambda i, j: (i, 0)),
            pl.BlockSpec((1, d), lambda i, j: (0, 0)),
            pl.BlockSpec((d, tn), lambda i, j: (0, j)),
            pl.BlockSpec((d, 128), lambda i, j: (0, 0)),
            pl.BlockSpec((n_gate, d), lambda i, j: (0, 0)),
            pl.BlockSpec((ROW_TILE, HEAD_DIM), lambda i, j: (i, 0)),
            pl.BlockSpec((ROW_TILE, HEAD_DIM), lambda i, j: (i, 0)),
        ],
        out_specs=(pl.BlockSpec((ROW_TILE, tn), lambda i, j: (i, j)),
                   pl.BlockSpec((ROW_TILE, 128), lambda i, j: (i, 0)),
                   pl.BlockSpec((n_gate, ROW_TILE), lambda i, j: (0, i))),
        scratch_shapes=[pltpu.VMEM((ROW_TILE, d), BF)],
        compiler_params=_params("parallel", "arbitrary"),
        name="q_proj",
    )(x, g.reshape(1, d), wq, wg_pad, wg_t, cos, sin)


CONV_PAD = 32


def _conv_kernel(buf_ref, u_ref, w_ref, bdw_ref, lg_ref, lb_ref, o_ref, full_ref, y_ref,
                 *, tt, n_t, row_chunk, col_chunk):
    lo = CONV_PAD - CONV_HALO
    if n_t > 1:
        t = pl.program_id(1)

        @pl.when(t == 0)
        def _():
            full_ref[:, lo:CONV_PAD, :] = buf_ref[...]

        @pl.when(t > 0)
        def _():
            full_ref[:, lo:CONV_PAD, :] = full_ref[:, tt + lo:tt + CONV_PAD, :]
    else:
        full_ref[:, lo:CONV_PAD, :] = buf_ref[...]
    full_ref[:, CONV_PAD:CONV_PAD + tt, :] = u_ref[...]

    d = u_ref.shape[-1]
    for r0 in range(0, tt, row_chunk):
        for c0 in range(0, d, col_chunk):
            cols = slice(c0, c0 + col_chunk)
            acc = None
            for k in range(CONV_WIDTH):
                term = w_ref[k:k + 1, cols] * full_ref[:, lo + k + r0:lo + k + r0 + row_chunk, cols]
                acc = term if acc is None else acc + term
            y_ref[:, r0:r0 + row_chunk, cols] = acc + bdw_ref[:, cols]

    y = y_ref[...]
    mu = jnp.mean(y, axis=-1, keepdims=True)
    yc = y - mu
    z = yc * lax.rsqrt(jnp.mean(yc * yc, axis=-1, keepdims=True) + EPS) * lg_ref[...] + lb_ref[...]
    o_ref[...] = z * _sigmoid(z)


def conv_ln(buf, u, wdw, bdw, ln_g, ln_b, *, bb, tt):
    b, t, d = u.shape
    n_t = t // tt
    row_chunk = min(tt, 64)
    kern = functools.partial(_conv_kernel, tt=tt, n_t=n_t, row_chunk=row_chunk, col_chunk=256)
    vec = lambda i, j: (0, 0)
    return pl.pallas_call(
        kern,
        out_shape=jax.ShapeDtypeStruct((b, t, d), F32),
        grid=(b // bb, n_t),
        in_specs=[
            pl.BlockSpec((bb, CONV_HALO, d), lambda i, j: (i, 0, 0)),
            pl.BlockSpec((bb, tt, d), lambda i, j: (i, j, 0)),
            pl.BlockSpec((CONV_WIDTH, d), vec),
            pl.BlockSpec((1, d), vec),
            pl.BlockSpec((1, d), vec),
            pl.BlockSpec((1, d), vec),
        ],
        out_specs=pl.BlockSpec((bb, tt, d), lambda i, j: (i, j, 0)),
        scratch_shapes=[pltpu.VMEM((bb, CONV_PAD + tt, d), F32), pltpu.VMEM((bb, tt, d), F32)],
        compiler_params=_params("parallel", "arbitrary"),
        name="conv_ln",
    )(buf, u, wdw, bdw.reshape(1, d), ln_g.reshape(1, d), ln_b.reshape(1, d))


def _pos_term_kernel(pe_ref, w_ref, o_ref):
    o_ref[0] = _dot(pe_ref[0], w_ref[0])


def pos_term(pe, w1):
    n = pe.shape[0]
    k = CMP_LEN * HEAD_DIM
    pe8 = jnp.broadcast_to(pe.reshape(n, 1, k), (n, 8, k)).astype(BF)
    return pl.pallas_call(
        _pos_term_kernel,
        out_shape=jax.ShapeDtypeStruct((n, 8, CMP_HIDDEN), F32),
        grid=(n,),
        in_specs=[pl.BlockSpec((1, 8, k), lambda i: (i, 0, 0)),
                  pl.BlockSpec((1, k, CMP_HIDDEN), lambda i: (i, 0, 0))],
        out_specs=pl.BlockSpec((1, 8, CMP_HIDDEN), lambda i: (i, 0, 0)),
        compiler_params=_params("parallel"),
        name="pos_term",
    )(pe8, w1.reshape(n, k, CMP_HIDDEN).astype(BF))


def _chunk_matrix(x, g):
    cols = [x[:, s * KV_COLS + g * HEAD_DIM: s * KV_COLS + (g + 1) * HEAD_DIM] for s in range(CMP_STRIDE)]
    return jnp.concatenate(cols, axis=1)


TILE_ROWS = 8
TILES_PER_CHUNK = CMP_STRIDE * KV_HEADS // TILE_ROWS
assert 2 * KV_HEADS == TILE_ROWS


def _page_chunk_rows(tile_ref):
    n_chunks = tile_ref.shape[1] // TILES_PER_CHUNK
    low = lax.broadcasted_iota(jnp.int32, (TILE_ROWS, HEAD_DIM), 0) < KV_HEADS
    cols = []
    for s in range(CMP_STRIDE):
        pieces = []
        for c in range(0, n_chunks, 2):
            a = tile_ref[0, TILES_PER_CHUNK * c + s // 2]
            b = tile_ref[0, TILES_PER_CHUNK * (c + 1) + s // 2]
            if s % 2 == 0:
                pieces.append(jnp.where(low, a, pltpu.roll(b, KV_HEADS, axis=0)))
            else:
                pieces.append(jnp.where(low, pltpu.roll(a, KV_HEADS, axis=0), b))
        cols.append(jnp.concatenate(pieces, axis=0))
    return jnp.concatenate(cols, axis=1)


def _compress_mlp(x, shift, w1_ref, pt_ref, w2_ref, b2_ref):
    n = x.shape[0]
    fs = _dot(x, w1_ref[0])
    first = fs[:, :CMP_HIDDEN]
    second = pltpu.roll(fs[:, CMP_HIDDEN:], n - shift, axis=0)
    pre = first + second + pt_ref[0, 0:1, :]
    hid = (pre * _sigmoid(pre)).astype(BF)
    out = _dot(hid, w2_ref[0]) + b2_ref[0]
    row = lax.broadcasted_iota(jnp.int32, (n, 1), 0)
    return jnp.where(row < n - shift, out, 0.0)


def _compress_rows_kernel(k_ref, v_ref, w1_ref, pt_ref, w2_ref, b2_ref, ok_ref, ov_ref):
    for i, (x_ref, o_ref) in enumerate(((k_ref, ok_ref), (v_ref, ov_ref))):
        w = (w1_ref.at[i:i + 1], pt_ref.at[i:i + 1], w2_ref.at[i:i + 1], b2_ref.at[i:i + 1])
        x = x_ref[0]
        for g in range(KV_HEADS):
            out = _compress_mlp(_chunk_matrix(x, g), 1, *w)
            o_ref[0, :, g * HEAD_DIM:(g + 1) * HEAD_DIM] = out.astype(o_ref.dtype)


def _compress_pages_kernel(*refs, n_pages):
    refs = refs[1:]
    w1_ref, pt_ref, w2_ref, b2_ref, ok_ref, ov_ref = refs[2 * n_pages:]
    for i, o_ref in enumerate((ok_ref, ov_ref)):
        w = (w1_ref.at[i:i + 1], pt_ref.at[i:i + 1], w2_ref.at[i:i + 1], b2_ref.at[i:i + 1])
        pages = refs[i * n_pages:(i + 1) * n_pages]
        x = jnp.concatenate([_page_chunk_rows(p).astype(BF) for p in pages], axis=0)
        o_ref[0] = _compress_mlp(x, KV_HEADS, *w).astype(o_ref.dtype)


def _compress_weight_specs():
    fixed3 = lambda *a: (0, 0, 0)
    return [
        pl.BlockSpec((2, CMP_STRIDE * HEAD_DIM, 2 * CMP_HIDDEN), fixed3),
        pl.BlockSpec((2, 8, CMP_HIDDEN), fixed3),
        pl.BlockSpec((2, CMP_HIDDEN, HEAD_DIM), fixed3),
        pl.BlockSpec((2, 1, HEAD_DIM), fixed3),
    ]


def compress_rows(k_rows, v_rows, cw):
    b, n, width = k_rows.shape
    spec = pl.BlockSpec((1, n, width), lambda i: (i, 0, 0))
    out = jax.ShapeDtypeStruct((b, n, KV_COLS), BF)
    ospec = pl.BlockSpec((1, n, KV_COLS), lambda i: (i, 0, 0))
    return pl.pallas_call(
        _compress_rows_kernel,
        out_shape=(out, out),
        grid=(b,),
        in_specs=[spec, spec] + _compress_weight_specs(),
        out_specs=(ospec, ospec),
        compiler_params=_params("parallel"),
        name="compress_rows",
    )(k_rows, v_rows, *cw)


def compress_pages(page_table, n_pages, cache_k, cache_v, cw):
    n_pool, tiles, rows, hd = cache_k.shape
    b = page_table.shape[0] // n_pages
    n = n_pages * tiles // TILES_PER_CHUNK * KV_HEADS

    def page_spec(p):
        return pl.BlockSpec((1, tiles, rows, hd), lambda i, pt: (pt[i * n_pages + p], 0, 0, 0))

    pages = [page_spec(p) for p in range(n_pages)]
    out = jax.ShapeDtypeStruct((b, n, HEAD_DIM), BF)
    ospec = pl.BlockSpec((1, n, HEAD_DIM), lambda i, pt: (i, 0, 0))
    grid_spec = pltpu.PrefetchScalarGridSpec(
        num_scalar_prefetch=1,
        grid=(b,),
        in_specs=pages + pages + _compress_weight_specs(),
        out_specs=(ospec, ospec),
    )
    return pl.pallas_call(
        functools.partial(_compress_pages_kernel, n_pages=n_pages),
        out_shape=(out, out),
        grid_spec=grid_spec,
        compiler_params=_params("parallel"),
        name="compress_pages",
    )(page_table, *([cache_k] * n_pages), *([cache_v] * n_pages), *cw)


def _softmax_cols(s, mask):
    s = jnp.where(mask, s, NEG_INF)
    m = jnp.max(s, axis=0, keepdims=True)
    m = jnp.where(m == NEG_INF, 0.0, m)
    p = jnp.where(mask, jnp.exp(s - m), 0.0)
    return p / jnp.maximum(jnp.sum(p, axis=0, keepdims=True), 1e-30)


def _top_rank_lanes(score, valid, n):
    idx = lax.broadcasted_iota(jnp.int32, score.shape, 1)
    cnt = jnp.zeros(score.shape, F32)
    for j in range(n):
        other = score[:, j:j + 1]
        tie = jnp.where(idx > j, 1.0, 0.0)
        cnt = cnt + jnp.where(other > score, 1.0, jnp.where(other == score, tie, 0.0))
    return jnp.where(cnt < TOP_N, jnp.where(valid, 1.0, 0.0), 0.0)


def _top_rank_sublanes(score, valid):
    n = score.shape[0]
    groups = [score[v:v + 8, :] for v in range(0, n, 8)]
    cnts = [jnp.zeros(g.shape, F32) for g in groups]
    for j in range(n):
        other = score[j:j + 1, :]
        for v, g in enumerate(groups):
            if 8 * v > j:
                beats = other >= g
            elif 8 * v + 7 <= j:
                beats = other > g
            else:
                sub = 8 * v + lax.broadcasted_iota(jnp.int32, g.shape, 0)
                beats = (other > g) | ((other == g) & (sub > j))
            cnts[v] = cnts[v] + jnp.where(beats, 1.0, 0.0)
    cnt = jnp.concatenate(cnts, axis=0)
    return jnp.where(cnt < TOP_N, jnp.where(valid, 1.0, 0.0), 0.0)


def _flash_step(q_rows, k, vt, bias, carry):
    m, l, acc = carry
    s = _dot_nt(k, q_rows) + bias
    m_new = jnp.maximum(m, jnp.max(s, axis=0, keepdims=True))
    m_safe = jnp.where(m_new == NEG_INF, 0.0, m_new)
    alpha = jnp.exp(m - m_safe)
    p = jnp.exp(s - m_safe)
    l = alpha * l + jnp.sum(p, axis=0, keepdims=True)
    acc = alpha * acc + _dot(vt, p.astype(BF))
    return m_new, l, acc


def _prompt_attn_kernel(q_ref, gt_ref, kc_ref, vct_ref, ks_ref, vst_ref, kw_ref, vwt_ref,
                        o_ref, sel_ref, *, n_cmp):
    qt = pl.program_id(2)
    n_rows = GROUP * Q_TILE
    q = q_ref[0]
    q_rows = jnp.concatenate([q[:, r * HEAD_DIM:(r + 1) * HEAD_DIM] for r in range(GROUP)], axis=0)
    t0 = qt * Q_TILE
    pos_t = t0 + lax.broadcasted_iota(jnp.int32, (1, Q_TILE), 1)
    pos_rows = jnp.concatenate([pos_t] * GROUP, axis=1)

    n_cpad = kc_ref.shape[1]
    ci = lax.broadcasted_iota(jnp.int32, (n_cpad, 1), 0)
    cmask = (ci * CMP_STRIDE + (CMP_LEN - 1) <= pos_rows) & (ci < n_cmp)
    p_cmp = _softmax_cols(_dot_nt(kc_ref[0], q_rows), cmask)
    o_cmp = _dot(vct_ref[0, 0], p_cmp.astype(BF))

    n_steps = ks_ref.shape[2]
    blk_per_step = KEY_STEP // SEL_BLOCK
    n_sel = n_steps * blk_per_step
    p_sum = p_cmp[:, 0:Q_TILE]
    for r in range(1, GROUP):
        p_sum = p_sum + p_cmp[:, r * Q_TILE:(r + 1) * Q_TILE]
    bj = lax.broadcasted_iota(jnp.int32, (n_sel, n_cpad), 0) * SEL_BLOCK
    bi = lax.broadcasted_iota(jnp.int32, (n_sel, n_cpad), 1) * CMP_STRIDE
    overlap_t = jnp.where((bi < bj + SEL_BLOCK) & (bi + CMP_LEN > bj), 1.0, 0.0).astype(BF)
    hi, mid, lo = _split3(p_sum)
    imp = _dot(overlap_t, hi) + _dot(overlap_t, mid) + _dot(overlap_t, lo)
    blk = lax.broadcasted_iota(jnp.int32, (n_sel, 1), 0)
    cur = pos_t >> SEL_SHIFT
    forced = (blk == 0) | (blk == cur) | (blk == cur - 1)
    causal = blk * SEL_BLOCK <= pos_t
    score = jnp.where(causal, imp + FORCE_BONUS * jnp.where(forced, 1.0, 0.0), NEG_INF)
    sel = _top_rank_sublanes(score, causal)
    for st in range(n_steps):
        sel_ref[st] = sel[st * blk_per_step:(st + 1) * blk_per_step, :]

    init = (jnp.full((1, n_rows), NEG_INF, F32), jnp.zeros((1, n_rows), F32),
            jnp.zeros((HEAD_DIM, n_rows), F32))

    key_in_blk = lax.broadcasted_iota(jnp.int32, (SEL_BLOCK, 1), 0)

    def sel_step(st, carry):
        picked = sel_ref[st]
        pieces = []
        for h in range(blk_per_step):
            kpos = st * KEY_STEP + h * SEL_BLOCK + key_in_blk
            ok = (kpos <= pos_t) & (picked[h:h + 1, :] > 0.5)
            pieces.append(jnp.where(ok, 0.0, NEG_INF))
        b = jnp.concatenate(pieces, axis=0)
        bias = jnp.concatenate([b] * GROUP, axis=1)
        return _flash_step(q_rows, ks_ref[0, 0, st], vst_ref[0, 0, st], bias, carry)

    n_needed = (t0 + Q_TILE + KEY_STEP - 1) // KEY_STEP
    _, l_sel, acc_sel = lax.fori_loop(0, n_needed, sel_step, init)
    o_sel = acc_sel / jnp.maximum(l_sel, 1e-30)

    n_wt = (WINDOW + Q_TILE) // KEY_TILE
    first = qt - (n_wt - 1)
    tiles = [jnp.maximum(first + i, 0) for i in range(n_wt)]
    k_win = jnp.concatenate([kw_ref[0, 0, kt] for kt in tiles], axis=0)
    vt_win = jnp.concatenate([vwt_ref[0, 0, kt] for kt in tiles], axis=1)
    kpos = first * KEY_TILE + lax.broadcasted_iota(jnp.int32, (n_wt * KEY_TILE, 1), 0)
    ok = (kpos >= 0) & (kpos <= pos_rows) & (pos_rows - kpos < WINDOW)
    _, l_win, acc_win = _flash_step(q_rows, k_win, vt_win, jnp.where(ok, 0.0, NEG_INF), init)
    o_win = acc_win / jnp.maximum(l_win, 1e-30)

    gt = gt_ref[0]
    for r in range(GROUP):
        cols = slice(r * Q_TILE, (r + 1) * Q_TILE)
        base = r * N_BRANCH
        mix = (gt[base:base + 1, :] * o_cmp[:, cols] + gt[base + 1:base + 2, :] * o_sel[:, cols]
               + gt[base + 2:base + 3, :] * o_win[:, cols])
        o_ref[0, :, r * HEAD_DIM:(r + 1) * HEAD_DIM] = mix.T.astype(o_ref.dtype)


def prompt_attention(q, gates_t, kc, vct, ks, vst, kw, vwt, n_cmp):
    b, t, _ = q.shape
    n_q = t // Q_TILE
    n_steps = ks.shape[2]
    n_kt = kw.shape[2]
    n_cpad = kc.shape[1]
    whole = lambda i, g, j: (i, g, 0, 0, 0)
    width = GROUP * HEAD_DIM
    return pl.pallas_call(
        functools.partial(_prompt_attn_kernel, n_cmp=n_cmp),
        out_shape=jax.ShapeDtypeStruct((b, t, N_HEADS * HEAD_DIM), BF),
        grid=(b, KV_HEADS, n_q),
        in_specs=[
            pl.BlockSpec((1, Q_TILE, width), lambda i, g, j: (i, j, g)),
            pl.BlockSpec((1, GROUP * N_BRANCH, Q_TILE), lambda i, g, j: (g, 0, i * n_q + j)),
            pl.BlockSpec((1, n_cpad, HEAD_DIM), lambda i, g, j: (i, 0, g)),
            pl.BlockSpec((1, 1, HEAD_DIM, n_cpad), lambda i, g, j: (i, g, 0, 0)),
            pl.BlockSpec((1, 1, n_steps, KEY_STEP, HEAD_DIM), whole),
            pl.BlockSpec((1, 1, n_steps, HEAD_DIM, KEY_STEP), whole),
            pl.BlockSpec((1, 1, n_kt, KEY_TILE, HEAD_DIM), whole),
            pl.BlockSpec((1, 1, n_kt, HEAD_DIM, KEY_TILE), whole),
        ],
        out_specs=pl.BlockSpec((1, Q_TILE, width), lambda i, g, j: (i, j, g)),
        scratch_shapes=[pltpu.VMEM((n_steps, KEY_STEP // SEL_BLOCK, Q_TILE), F32)],
        compiler_params=_params("parallel", "parallel", "arbitrary"),
        name="prompt_attention",
    )(q, gates_t, kc, vct, ks, vst, kw, vwt)


def _softmax_two(s_a, bias_a, s_b, bias_b):
    s_a = s_a + bias_a
    s_b = s_b + bias_b
    m = jnp.maximum(jnp.max(s_a, axis=1, keepdims=True), jnp.max(s_b, axis=1, keepdims=True))
    m = jnp.where(m == NEG_INF, 0.0, m)
    p_a = jnp.exp(s_a - m)
    p_b = jnp.exp(s_b - m)
    total = jnp.sum(p_a, axis=1, keepdims=True) + jnp.sum(p_b, axis=1, keepdims=True)
    return p_a, p_b, 1.0 / jnp.maximum(total, 1e-30)


def _sample_attn_kernel(*refs, n_pages, t_new, past_len, n_cmp):
    refs = refs[1:]
    q_ref, gate_ref, kc_ref, vc_ref = refs[:4]
    ks_pages = refs[4:4 + n_pages]
    vs_pages = refs[4 + n_pages:4 + 2 * n_pages]
    (ksn_ref, vsn_ref, kwp_ref, vwp_ref, kwn_ref, vwn_ref, o_ref,
     expand_ref, past_bias_ref, win_bias_ref) = refs[4 + 2 * n_pages:]

    n_rows = q_ref.shape[1]
    rows_per_head = n_rows // KV_HEADS
    q = q_ref[0]
    row = lax.broadcasted_iota(jnp.int32, (n_rows, 1), 0)
    row_head = _div(row, rows_per_head)
    pos = past_len + _mod(row, t_new)
    n_past = n_pages * ks_pages[0].shape[1]
    n_win = kwp_ref.shape[1]

    def col_key(n_cols):
        return _div(lax.broadcasted_iota(jnp.int32, (1, n_cols), 1), KV_HEADS)

    def head_bias(n_cols):
        col = lax.broadcasted_iota(jnp.int32, (1, n_cols), 1)
        return jnp.where(_mod(col, KV_HEADS) == row_head, 0.0, NEG_INF)

    @pl.when(pl.program_id(0) == 0)
    def _():
        ej = lax.broadcasted_iota(jnp.int32, (128, n_past), 0)
        ek = _div(lax.broadcasted_iota(jnp.int32, (128, n_past), 1), KV_HEADS)
        expand_ref[...] = jnp.where((ek >> SEL_SHIFT) == ej, 1.0, 0.0).astype(BF)
        past_bias_ref[...] = jnp.where(col_key(n_past) <= pos, head_bias(n_past), NEG_INF)
        wpos = past_len - n_win // KV_HEADS + col_key(n_win)
        win_bias_ref[...] = jnp.where((wpos <= pos) & (pos - wpos < WINDOW), head_bias(n_win), NEG_INF)

    new_pos = past_len + col_key(128)
    new_bias = jnp.where((col_key(128) < t_new) & (new_pos <= pos), head_bias(128), NEG_INF)

    def pad_new(ref):
        x = ref[0]
        return jnp.concatenate([x, jnp.zeros((128 - x.shape[0], x.shape[1]), x.dtype)], axis=0).astype(BF)

    n_c = kc_ref.shape[1]
    chunk = col_key(n_c)
    c_ok = (chunk * CMP_STRIDE + (CMP_LEN - 1) <= pos) & (chunk < n_cmp)
    s_c = _dot_nt(q, kc_ref[0]) + jnp.where(c_ok, head_bias(n_c), NEG_INF)
    m_c = jnp.max(s_c, axis=1, keepdims=True)
    m_c = jnp.where(m_c == NEG_INF, 0.0, m_c)
    p_c = jnp.exp(s_c - m_c)
    p_c = p_c / jnp.maximum(jnp.sum(p_c, axis=1, keepdims=True), 1e-30)
    o_cmp = _dot(p_c.astype(BF), vc_ref[0])

    ra = lax.broadcasted_iota(jnp.int32, (n_rows, n_rows), 0)
    rb = lax.broadcasted_iota(jnp.int32, (n_rows, n_rows), 1)
    same_gt = ((_div(ra, rows_per_head) == _div(rb, rows_per_head))
               & (_mod(ra, t_new) == _mod(rb, t_new)))
    group_sum = jnp.where(same_gt, 1.0, 0.0).astype(BF)
    hi, mid, lo = _split3(p_c)
    p_sum = _dot(group_sum, hi) + _dot(group_sum, mid) + _dot(group_sum, lo)
    n_keys = past_len + t_new
    n_sel = -(-n_keys // SEL_BLOCK)
    bi = _div(lax.broadcasted_iota(jnp.int32, (n_c, 128), 0), KV_HEADS) * CMP_STRIDE
    bj = lax.broadcasted_iota(jnp.int32, (n_c, 128), 1) * SEL_BLOCK
    overlap = jnp.where((bi < bj + SEL_BLOCK) & (bi + CMP_LEN > bj), 1.0, 0.0).astype(BF)
    imp = _dot01(p_sum, overlap)
    blk = lax.broadcasted_iota(jnp.int32, (1, 128), 1)
    cur = pos >> SEL_SHIFT
    forced = (blk == 0) | (blk == cur) | (blk == cur - 1)
    causal = (blk * SEL_BLOCK <= pos) & (blk < n_sel)
    score = jnp.where(causal, imp + FORCE_BONUS * jnp.where(forced, 1.0, 0.0), NEG_INF)
    sel = _top_rank_lanes(score, causal, n_sel)

    k_past = jnp.concatenate([p[0].astype(BF) for p in ks_pages], axis=0)
    v_past = jnp.concatenate([p[0].astype(BF) for p in vs_pages], axis=0)
    sel16 = sel.astype(BF)
    bias_past = jnp.where(_dot(sel16, expand_ref[...]) > 0.5, past_bias_ref[...], NEG_INF)
    nj = lax.broadcasted_iota(jnp.int32, (128, 128), 0)
    nk = past_len + _div(lax.broadcasted_iota(jnp.int32, (128, 128), 1), KV_HEADS)
    expand_new = jnp.where((nk >> SEL_SHIFT) == nj, 1.0, 0.0).astype(BF)
    bias_new = jnp.where(_dot(sel16, expand_new) > 0.5, new_bias, NEG_INF)
    p_a, p_b, inv = _softmax_two(_dot_nt(q, k_past), bias_past, _dot_nt(q, pad_new(ksn_ref)), bias_new)
    o_sel = (_dot(p_a.astype(BF), v_past) + _dot(p_b.astype(BF), pad_new(vsn_ref))) * inv

    bias_wn = jnp.where(pos - new_pos < WINDOW, new_bias, NEG_INF)
    p_a, p_b, inv = _softmax_two(_dot_nt(q, kwp_ref[0].astype(BF)), win_bias_ref[...],
                                 _dot_nt(q, pad_new(kwn_ref)), bias_wn)
    o_win = (_dot(p_a.astype(BF), vwp_ref[0].astype(BF)) + _dot(p_b.astype(BF), pad_new(vwn_ref))) * inv

    gate = gate_ref[0]
    o = gate[:, 0:1] * o_cmp + gate[:, 1:2] * o_sel + gate[:, 2:3] * o_win
    o_ref[0] = o.astype(o_ref.dtype)


def sample_attention(page_table, q, gates, kc, vc, cache_ks, cache_vs, ks_new, vs_new,
                     kw_past, vw_past, kw_new, vw_new, *, t_new, n_cmp):
    b, n_rows, _ = q.shape
    n_pages = page_table.shape[0] // b
    page_rows = cache_ks.shape[1]
    n_c = kc.shape[1]
    n_win = kw_past.shape[1]
    n_new = ks_new.shape[1]
    n_past = n_pages * page_rows

    def page_spec(p):
        return pl.BlockSpec((1, page_rows, HEAD_DIM), lambda i, pt: (pt[i * n_pages + p], 0, 0))

    per_seq = lambda rows, cols=HEAD_DIM: pl.BlockSpec((1, rows, cols), lambda i, pt: (i, 0, 0))
    pages = [page_spec(p) for p in range(n_pages)]
    grid_spec = pltpu.PrefetchScalarGridSpec(
        num_scalar_prefetch=1,
        grid=(b,),
        in_specs=[per_seq(n_rows), per_seq(n_rows, N_BRANCH), per_seq(n_c), per_seq(n_c)]
        + pages + pages
        + [per_seq(n_new), per_seq(n_new), per_seq(n_win), per_seq(n_win), per_seq(n_new), per_seq(n_new)],
        out_specs=per_seq(n_rows),
        scratch_shapes=[pltpu.VMEM((128, n_past), BF), pltpu.VMEM((n_rows, n_past), F32),
                        pltpu.VMEM((n_rows, n_win), F32)],
    )
    kern = functools.partial(_sample_attn_kernel, n_pages=n_pages, t_new=t_new,
                             past_len=n_past // KV_HEADS, n_cmp=n_cmp)
    return pl.pallas_call(
        kern,
        out_shape=jax.ShapeDtypeStruct((b, n_rows, HEAD_DIM), BF),
        grid_spec=grid_spec,
        compiler_params=_params("arbitrary"),
        name="sample_attention",
    )(page_table, q, gates, kc, vc, *([cache_ks] * n_pages), *([cache_vs] * n_pages),
      ks_new, vs_new, kw_past, vw_past, kw_new, vw_new)


def _rope_tables(pos):
    half = HEAD_DIM // 2
    inv = ROPE_THETA ** (-jnp.arange(half, dtype=F32) / half)
    ang = pos.astype(F32)[:, None] * inv[None, :]
    cos = jnp.cos(ang)
    sin = jnp.sin(ang)
    return jnp.concatenate([cos, cos], axis=1), jnp.concatenate([-sin, sin], axis=1)


def kernel(x_prompt, x_sample, cache_k_cmp, cache_v_cmp, cache_k_sel, cache_v_sel, state_k_win, state_v_win, state_conv, page_table, norm_g, conv_w1, conv_b1, conv_wdw, conv_bdw, conv_ln_g, conv_ln_b, conv_w2, conv_b2, kv_norm_g, w_kv, cmp_w1, cmp_pe, cmp_w2, cmp_b2, nsa_w_qg, nsa_w_o, ffn_w_in, ffn_w_out):
    bp, tp, d = x_prompt.shape
    bs, ts, _ = x_sample.shape
    n_p = bp * tp
    n_s = bs * ts
    n_pool, page, _, _ = cache_k_sel.shape
    n_pages = page_table.shape[1]
    past_len = n_pages * page
    bf = lambda w: w.astype(BF)

    def split(a):
        return a[:n_p], a[n_p:]

    x = jnp.concatenate([x_prompt.reshape(n_p, d), x_sample.reshape(n_s, d)], axis=0)
    pos = jnp.concatenate([jnp.tile(jnp.arange(tp, dtype=jnp.int32), bp),
                           jnp.tile(past_len + jnp.arange(ts, dtype=jnp.int32), bs)])
    cos, sin = _rope_tables(pos)

    g = norm_g[0]
    u = glu_in(x, g[0], bf(conv_w1[0]), conv_b1[0])
    u_p, u_s = split(u)
    u_p = u_p.reshape(bp, tp, d)
    u_s = u_s.reshape(bs, ts, d)
    conv_args = (conv_wdw[0], conv_bdw[0], conv_ln_g[0], conv_ln_b[0])
    v_p = conv_ln(jnp.zeros((bp, CONV_HALO, d), F32), u_p, *conv_args, bb=1, tt=128)
    v_s = conv_ln(state_conv[0], u_s, *conv_args, bb=16, tt=ts)
    v = jnp.concatenate([v_p.reshape(n_p, d), v_s.reshape(n_s, d)], axis=0)
    x = mm_norm_res(v, bf(conv_w2[0]), conv_b2[0], g[1], x)
    ffn_in16, ffn_out16 = bf(ffn_w_in), bf(ffn_w_out)
    x = ffn(x, g[2], ffn_in16, ffn_out16, g[3], 0)
    conv_p = u_p[:, tp - CONV_HALO:][None]
    conv_s = jnp.concatenate([state_conv[0][:, ts:], u_s], axis=1)[None]

    kv32, kv16 = kv_proj(x, kv_norm_g, bf(w_kv), cos, sin)
    rows_p = [kv32[s, :n_p].reshape(bp, tp, KV_HEADS, HEAD_DIM) for s in range(N_KV_SLOTS)]
    rows_s = [kv32[s, n_p:].reshape(bs, ts, KV_HEADS, HEAD_DIM) for s in range(N_KV_SLOTS)]
    n_wp = min(WINDOW, tp)
    kw_p, vw_p = rows_p[4][:, tp - n_wp:], rows_p[5][:, tp - n_wp:]
    kw_s = jnp.concatenate([state_k_win[:, ts:], rows_s[4]], axis=1)
    vw_s = jnp.concatenate([state_v_win[:, ts:], rows_s[5]], axis=1)

    w1 = cmp_w1.reshape(2, 2, CMP_STRIDE * HEAD_DIM, CMP_HIDDEN)
    w1cat = bf(jnp.concatenate([w1[:, 0], w1[:, 1]], axis=-1))
    cw = (w1cat, pos_term(cmp_pe, cmp_w1), bf(cmp_w2), cmp_b2.reshape(2, 1, HEAD_DIM))
    chunk_cols = CMP_STRIDE * KV_COLS
    kc_p, vc_p = compress_rows(kv16[0, :n_p].reshape(bp, tp // CMP_STRIDE, chunk_cols),
                               kv16[1, :n_p].reshape(bp, tp // CMP_STRIDE, chunk_cols), cw)
    pt_flat = page_table.reshape(-1)
    assert (past_len + ts) // CMP_STRIDE == past_len // CMP_STRIDE
    page_rows = page * KV_HEADS
    as_tiles = lambda c: c.reshape(n_pool, page_rows // TILE_ROWS, TILE_ROWS, HEAD_DIM)
    kc_s, vc_s = compress_pages(pt_flat, n_pages, as_tiles(cache_k_cmp), as_tiles(cache_v_cmp), cw)

    g = norm_g[1]
    w_qg = nsa_w_qg[0]
    n_q_cols = N_HEADS * HEAD_DIM
    n_gate = N_HEADS * N_BRANCH
    wg = w_qg[:, n_q_cols:]
    wg_pad = bf(jnp.pad(wg, ((0, 0), (0, 128 - n_gate))))
    q, gates, gates_t = q_proj(x, g[0], bf(w_qg[:, :n_q_cols]), wg_pad, bf(wg.T), cos, sin)

    assert tp % KEY_STEP == 0 and tp % Q_TILE == 0

    def key_tiles(a, tk):
        return a.reshape(bp, tp // tk, tk, KV_HEADS, HEAD_DIM).transpose(0, 3, 1, 2, 4)

    def value_tiles_t(a, tk):
        return a.reshape(bp, tp // tk, tk, KV_HEADS, HEAD_DIM).transpose(0, 3, 1, 4, 2)

    n_cmp_p = tp // CMP_STRIDE - 1
    vct_p = vc_p.reshape(bp, tp // CMP_STRIDE, KV_HEADS, HEAD_DIM).transpose(0, 2, 3, 1)
    gates_t_p = gates_t[:, :n_p].reshape(KV_HEADS, GROUP * N_BRANCH, n_p)
    o_p = prompt_attention(q[:n_p].reshape(bp, tp, n_q_cols), gates_t_p, kc_p, vct_p,
                           key_tiles(kv16[2, :n_p], KEY_STEP), value_tiles_t(kv16[3, :n_p], KEY_STEP),
                           key_tiles(kv16[4, :n_p], KEY_TILE), value_tiles_t(kv16[5, :n_p], KEY_TILE), n_cmp_p)

    def seq_rows(a, width):
        return a.reshape(bs, ts, KV_HEADS, GROUP, width).transpose(0, 2, 3, 1, 4).reshape(bs, -1, width)

    q_s = seq_rows(q[n_p:], HEAD_DIM)
    gates_s = seq_rows(gates[n_p:, :n_gate], N_BRANCH)

    def new_rows(slot):
        return kv32[slot, n_p:].reshape(bs, ts * KV_HEADS, HEAD_DIM)

    head_rows = lambda a: a.reshape(a.shape[0], -1, HEAD_DIM)
    o_s = sample_attention(
        pt_flat, q_s, gates_s, kc_s, vc_s,
        head_rows(cache_k_sel), head_rows(cache_v_sel), new_rows(2), new_rows(3),
        head_rows(state_k_win), head_rows(state_v_win), new_rows(4), new_rows(5), t_new=ts, n_cmp=past_len // CMP_STRIDE - 1)
    o_s = o_s.reshape(bs, KV_HEADS, GROUP, ts, HEAD_DIM).transpose(0, 3, 1, 2, 4).reshape(n_s, n_q_cols)
    o = jnp.concatenate([o_p.reshape(n_p, n_q_cols), o_s], axis=0)
    x = mm_norm_res(o, bf(nsa_w_o[0]), None, g[1], x)
    x = ffn(x, g[2], ffn_in16, ffn_out16, g[3], 1)

    y_p, y_s = split(x)
    return (y_p.reshape(bp, tp, d), y_s.reshape(bs, ts, d), conv_p,
            rows_p[0], rows_p[1], rows_p[2], rows_p[3], kw_p, vw_p,
            conv_s, rows_s[0], rows_s[1], rows_s[2], rows_s[3], kw_s, vw_s)
```

```python
import functools

import jax
import jax.numpy as jnp
from jax import lax
from jax.experimental import pallas as pl
from jax.experimental.pallas import tpu as pltpu

D_MODEL = 2048
CONV_WIDTH = 31
CONV_HALO = CONV_WIDTH - 1
N_HEADS = 16
HEAD_DIM = 128
KV_HEADS = 4
GROUP = N_HEADS // KV_HEADS
N_BRANCH = 3
N_KV_SLOTS = 6
KV_COLS = KV_HEADS * HEAD_DIM
CMP_LEN = 32
CMP_STRIDE = 16
CMP_HIDDEN = 256
SEL_BLOCK = 64
TOP_N = 16
WINDOW = 512
ROPE_THETA = 10000.0
FORCE_BONUS = 1.0e4
EPS = 1e-6
SCALE = HEAD_DIM ** -0.5
SEL_SHIFT = SEL_BLOCK.bit_length() - 1
assert 1 << SEL_SHIFT == SEL_BLOCK

TILE_ROWS = 8
BF = jnp.bfloat16
F32 = jnp.float32
NEG_INF = float("-inf")

VMEM_LIMIT_BYTES = 56 * 1024 * 1024
ROW_TILE = 512
KEY_TILE = 128
KEY_STEP = 512
Q_TILE = 128


def _params(*sem):
    return pltpu.CompilerParams(dimension_semantics=sem, vmem_limit_bytes=VMEM_LIMIT_BYTES)


def _sigmoid(x):
    return 1.0 / (1.0 + jnp.exp(-x))


def _rms(x, g):
    return x * lax.rsqrt(jnp.mean(x * x, axis=-1, keepdims=True) + EPS) * g


def _rope(y, cos, sin_signed):
    return y * cos + pltpu.roll(y, HEAD_DIM // 2, axis=1) * sin_signed


def _split3(x):
    hi = x.astype(BF)
    r1 = x - hi.astype(F32)
    mid = r1.astype(BF)
    lo = (r1 - mid.astype(F32)).astype(BF)
    return hi, mid, lo


def _div(x, n):
    assert n & (n - 1) == 0
    return x >> (n.bit_length() - 1)


def _mod(x, n):
    assert n & (n - 1) == 0
    return x & (n - 1)


def _dot(a, b):
    return jnp.dot(a, b, preferred_element_type=F32)


def _dot_nt(a, b):
    return lax.dot_general(a, b, (((1,), (1,)), ((), ())), preferred_element_type=F32)


def _dot01(x, onehot):
    hi, mid, lo = _split3(x)
    return _dot(hi, onehot) + _dot(mid, onehot) + _dot(lo, onehot)


def _row_pair_specs(n_first, block):
    first = pl.BlockSpec(block, lambda i, *_: (jnp.minimum(i, n_first - 1), 0))
    second = pl.BlockSpec(block, lambda i, *_: (jnp.maximum(i - n_first, 0), 0))
    return [first, second]


def _glu_in_kernel(xa_ref, xb_ref, g_ref, wa_ref, wb_ref, ba_ref, bb_ref, o_ref, xn_ref, *, n_first):
    i = pl.program_id(0)
    j = pl.program_id(1)

    @pl.when((j == 0) & (i < n_first))
    def _():
        xn_ref[...] = _rms(xa_ref[...], g_ref[...]).astype(BF)

    @pl.when((j == 0) & (i >= n_first))
    def _():
        xn_ref[...] = _rms(xb_ref[...], g_ref[...]).astype(BF)

    xn = xn_ref[...]
    a = _dot(xn, wa_ref[...]) + ba_ref[...]
    b = _dot(xn, wb_ref[...]) + bb_ref[...]
    o_ref[...] = a * _sigmoid(b)


def glu_in(x_pair, g, w, b):
    xa, xb = x_pair
    d = xa.shape[1]
    m = xa.shape[0] + xb.shape[0]
    n_first = xa.shape[0] // ROW_TILE
    tn = 512
    nj = d // tn
    return pl.pallas_call(
        functools.partial(_glu_in_kernel, n_first=n_first),
        out_shape=jax.ShapeDtypeStruct((m, d), F32),
        grid=(m // ROW_TILE, nj),
        in_specs=_row_pair_specs(n_first, (ROW_TILE, d)) + [
            pl.BlockSpec((1, d), lambda i, j: (0, 0)),
            pl.BlockSpec((d, tn), lambda i, j: (0, j)),
            pl.BlockSpec((d, tn), lambda i, j: (0, j + nj)),
            pl.BlockSpec((1, tn), lambda i, j: (0, j)),
            pl.BlockSpec((1, tn), lambda i, j: (0, j + nj)),
        ],
        out_specs=pl.BlockSpec((ROW_TILE, tn), lambda i, j: (i, j)),
        scratch_shapes=[pltpu.VMEM((ROW_TILE, d), BF)],
        compiler_params=_params("parallel", "arbitrary"),
        name="glu_in",
    )(xa, xb, g.reshape(1, d), w, w, b.reshape(1, 2 * d), b.reshape(1, 2 * d))


def _mm_norm_res_kernel(*refs, has_bias, x_is_pair, n_first):
    ha_ref, hb_ref, w_ref = refs[:3]
    refs = refs[3:]
    if has_bias:
        b_ref, refs = refs[0], refs[1:]
    g_ref, refs = refs[0], refs[1:]
    x_refs, (o_ref, h_ref) = refs[:-2], refs[-2:]
    is_first = pl.program_id(0) < n_first

    @pl.when(is_first)
    def _():
        h_ref[...] = ha_ref[...].astype(BF)

    @pl.when(jnp.logical_not(is_first))
    def _():
        h_ref[...] = hb_ref[...].astype(BF)

    y = _dot(h_ref[...], w_ref[...])
    if has_bias:
        y = y + b_ref[...]
    r = _rms(y, g_ref[...])
    if x_is_pair:
        @pl.when(is_first)
        def _():
            o_ref[...] = x_refs[0][...] + r

        @pl.when(jnp.logical_not(is_first))
        def _():
            o_ref[...] = x_refs[1][...] + r
    else:
        o_ref[...] = x_refs[0][...] + r


def mm_norm_res(h_pair, w, b, g, x):
    ha, hb = h_pair
    k = ha.shape[1]
    m = ha.shape[0] + hb.shape[0]
    d = w.shape[1]
    tm = 256
    n_first = ha.shape[0] // tm
    row = lambda i: (i, 0)
    fixed = lambda i: (0, 0)
    in_specs = _row_pair_specs(n_first, (tm, k)) + [pl.BlockSpec((k, d), fixed)]
    args = [ha, hb, w]
    if b is not None:
        in_specs.append(pl.BlockSpec((1, d), fixed))
        args.append(b.reshape(1, d))
    in_specs.append(pl.BlockSpec((1, d), fixed))
    args.append(g.reshape(1, d))
    x_is_pair = isinstance(x, tuple)
    if x_is_pair:
        in_specs += _row_pair_specs(n_first, (tm, d))
        args += list(x)
    else:
        in_specs.append(pl.BlockSpec((tm, d), row))
        args.append(x)
    return pl.pallas_call(
        functools.partial(_mm_norm_res_kernel, has_bias=b is not None, x_is_pair=x_is_pair, n_first=n_first),
        out_shape=jax.ShapeDtypeStruct((m, d), F32),
        grid=(m // tm,),
        in_specs=in_specs,
        out_specs=pl.BlockSpec((tm, d), row),
        scratch_shapes=[pltpu.VMEM((tm, k), BF)],
        compiler_params=_params("parallel"),
        name="mm_norm_res",
    )(*args)


def _ffn_kernel(x_ref, gi_ref, wa_ref, wb_ref, wo_ref, go_ref, o_ref, xn_ref, acc_ref):
    j = pl.program_id(1)

    @pl.when(j == 0)
    def _():
        xn_ref[...] = _rms(x_ref[...], gi_ref[...]).astype(BF)
        acc_ref[...] = jnp.zeros_like(acc_ref)

    xn = xn_ref[...]
    a = _dot(xn, wa_ref[0])
    b = _dot(xn, wb_ref[0])
    h = (a * _sigmoid(a) * b).astype(BF)
    acc_ref[...] += _dot(h, wo_ref[0])

    @pl.when(j == pl.num_programs(1) - 1)
    def _():
        o_ref[...] = x_ref[...] + _rms(acc_ref[...], go_ref[...])


def ffn(x, g_in, w_in, w_out, g_out, layer):
    m, d = x.shape
    f = w_out.shape[1]
    tf = 512
    nj = f // tf
    return pl.pallas_call(
        _ffn_kernel,
        out_shape=jax.ShapeDtypeStruct((m, d), F32),
        grid=(m // ROW_TILE, nj),
        in_specs=[
            pl.BlockSpec((ROW_TILE, d), lambda i, j: (i, 0)),
            pl.BlockSpec((1, d), lambda i, j: (0, 0)),
            pl.BlockSpec((1, d, tf), lambda i, j: (layer, 0, j)),
            pl.BlockSpec((1, d, tf), lambda i, j: (layer, 0, j + nj)),
            pl.BlockSpec((1, tf, d), lambda i, j: (layer, j, 0)),
            pl.BlockSpec((1, d), lambda i, j: (0, 0)),
        ],
        out_specs=pl.BlockSpec((ROW_TILE, d), lambda i, j: (i, 0)),
        scratch_shapes=[pltpu.VMEM((ROW_TILE, d), BF), pltpu.VMEM((ROW_TILE, d), F32)],
        compiler_params=_params("parallel", "arbitrary"),
        name="ffn",
    )(x, g_in.reshape(1, d), w_in, w_in, w_out, g_out.reshape(1, d))


def _kv_proj_kernel(x_ref, g_ref, w_ref, cos_ref, sin_ref, o32_ref, o16_ref, xn_ref):
    s = pl.program_id(1)

    @pl.when(s == 0)
    def _():
        xn_ref[...] = _rms(x_ref[...], g_ref[...]).astype(BF)

    y = _dot(xn_ref[...], w_ref[...])

    @pl.when(s % 2 == 0)
    def _():
        cos = cos_ref[...]
        sin = sin_ref[...]
        for h in range(KV_HEADS):
            cols = slice(h * HEAD_DIM, (h + 1) * HEAD_DIM)
            r = _rope(y[:, cols], cos, sin)
            o32_ref[0, :, cols] = r
            o16_ref[0, :, cols] = r.astype(BF)

    @pl.when(s % 2 == 1)
    def _():
        o32_ref[0] = y
        o16_ref[0] = y.astype(BF)


def kv_proj(x, g, w, cos, sin):
    m, d = x.shape
    out = jax.ShapeDtypeStruct((N_KV_SLOTS, m, KV_COLS), F32)
    out16 = jax.ShapeDtypeStruct((N_KV_SLOTS, m, KV_COLS), BF)
    return pl.pallas_call(
        _kv_proj_kernel,
        out_shape=(out, out16),
        grid=(m // ROW_TILE, N_KV_SLOTS),
        in_specs=[
            pl.BlockSpec((ROW_TILE, d), lambda i, s: (i, 0)),
            pl.BlockSpec((1, d), lambda i, s: (0, 0)),
            pl.BlockSpec((d, KV_COLS), lambda i, s: (0, s)),
            pl.BlockSpec((ROW_TILE, HEAD_DIM), lambda i, s: (i, 0)),
            pl.BlockSpec((ROW_TILE, HEAD_DIM), lambda i, s: (i, 0)),
        ],
        out_specs=(pl.BlockSpec((1, ROW_TILE, KV_COLS), lambda i, s: (s, i, 0)),
                   pl.BlockSpec((1, ROW_TILE, KV_COLS), lambda i, s: (s, i, 0))),
        scratch_shapes=[pltpu.VMEM((ROW_TILE, d), BF)],
        compiler_params=_params("parallel", "arbitrary"),
        name="kv_proj",
    )(x, g.reshape(1, d), w, cos, sin)


def _q_proj_kernel(x_ref, g_ref, wq_ref, wg_ref, wgt_ref, cos_ref, sin_ref,
                   q_ref, gate_ref, gatet_ref, xn_ref):
    @pl.when(pl.program_id(1) == 0)
    def _():
        xn = _rms(x_ref[...], g_ref[...]).astype(BF)
        xn_ref[...] = xn
        gate_ref[...] = _sigmoid(_dot(xn, wg_ref[...]))
        gatet_ref[...] = _sigmoid(_dot_nt(wgt_ref[...], xn))

    y = _dot(xn_ref[...], wq_ref[...])
    cos = cos_ref[...]
    sin = sin_ref[...]
    for h in range(GROUP):
        cols = slice(h * HEAD_DIM, (h + 1) * HEAD_DIM)
        q_ref[:, cols] = (_rope(y[:, cols], cos, sin) * SCALE).astype(BF)


def q_proj(x, g, wq, wg_pad, wg_t, cos, sin):
    m, d = x.shape
    n_gate = wg_t.shape[0]
    tn = GROUP * HEAD_DIM
    return pl.pallas_call(
        _q_proj_kernel,
        out_shape=(jax.ShapeDtypeStruct((m, N_HEADS * HEAD_DIM), BF),
                   jax.ShapeDtypeStruct((m, 128), F32),
                   jax.ShapeDtypeStruct((n_gate, m), F32)),
        grid=(m // ROW_TILE, KV_HEADS),
        in_specs=[
            pl.BlockSpec((ROW_TILE, d), lambda i, j: (i, 0)),
            pl.BlockSpec((1, d), lambda i, j: (0, 0)),
            pl.BlockSpec((d, tn), lambda i, j: (0, j)),
            pl.BlockSpec((d, 128), lambda i, j: (0, 0)),
            pl.BlockSpec((n_gate, d), lambda i, j: (0, 0)),
            pl.BlockSpec((ROW_TILE, HEAD_DIM), lambda i, j: (i, 0)),
            pl.BlockSpec((ROW_TILE, HEAD_DIM), lambda i, j: (i, 0)),
        ],
        out_specs=(pl.BlockSpec((ROW_TILE, tn), lambda i, j: (i, j)),
                   pl.BlockSpec((ROW_TILE, 128), lambda i, j: (i, 0)),
                   pl.BlockSpec((n_gate, ROW_TILE), lambda i, j: (0, i))),
        scratch_shapes=[pltpu.VMEM((ROW_TILE, d), BF)],
        compiler_params=_params("parallel", "arbitrary"),
        name="q_proj",
    )(x, g.reshape(1, d), wq, wg_pad, wg_t, cos, sin)


CONV_PAD = 32


def _conv_kernel(buf_ref, u_ref, w_ref, bdw_ref, lg_ref, lb_ref, o_ref, full_ref, y_ref,
                 *, tt, n_t, col_chunk):
    lo = CONV_PAD - CONV_HALO
    if n_t > 1:
        t = pl.program_id(1)

        @pl.when(t == 0)
        def _():
            full_ref[:, lo:CONV_PAD, :] = buf_ref[...]

        @pl.when(t > 0)
        def _():
            full_ref[:, lo:CONV_PAD, :] = full_ref[:, tt + lo:tt + CONV_PAD, :]
    else:
        full_ref[:, lo:CONV_PAD, :] = buf_ref[...]
    full_ref[:, CONV_PAD:CONV_PAD + tt, :] = u_ref[...]

    d = u_ref.shape[-1]
    if tt % TILE_ROWS == 0:
        full_ref[:, CONV_PAD + tt:, :] = jnp.zeros((full_ref.shape[0], TILE_ROWS, d), F32)
        for c0 in range(0, d, col_chunk):
            cols = slice(c0, c0 + col_chunk)
            acc = None
            for rem in range(TILE_ROWS):
                z = None
                for k in range(CONV_WIDTH):
                    if (lo + k) % TILE_ROWS != rem:
                        continue
                    base = lo + k - rem
                    term = w_ref[k:k + 1, cols] * full_ref[:, base:base + tt + TILE_ROWS, cols]
                    z = term if z is None else z + term
                z = z[:, rem:rem + tt, :]
                acc = z if acc is None else acc + z
            y_ref[:, :, cols] = acc + bdw_ref[:, cols]
    else:
        for c0 in range(0, d, col_chunk):
            cols = slice(c0, c0 + col_chunk)
            acc = None
            for k in range(CONV_WIDTH):
                term = w_ref[k:k + 1, cols] * full_ref[:, lo + k:lo + k + tt, cols]
                acc = term if acc is None else acc + term
            y_ref[:, :, cols] = acc + bdw_ref[:, cols]

    y = y_ref[...]
    mu = jnp.mean(y, axis=-1, keepdims=True)
    yc = y - mu
    z = yc * lax.rsqrt(jnp.mean(yc * yc, axis=-1, keepdims=True) + EPS) * lg_ref[...] + lb_ref[...]
    o_ref[...] = z * _sigmoid(z)


def conv_ln(buf, u, wdw, bdw, ln_g, ln_b, *, bb, tt, n_t=1):
    b, _, d = buf.shape
    aligned = tt % TILE_ROWS == 0
    kern = functools.partial(_conv_kernel, tt=tt, n_t=n_t, col_chunk=128 if aligned else 256)
    full_rows = CONV_PAD + tt + (TILE_ROWS if aligned else 0)
    vec = lambda i, j: (0, 0)
    tile = lambda i, j: (i * n_t + j, 0, 0)
    return pl.pallas_call(
        kern,
        out_shape=jax.ShapeDtypeStruct((b * n_t, tt, d), F32),
        grid=(b // bb, n_t),
        in_specs=[
            pl.BlockSpec((bb, CONV_HALO, d), lambda i, j: (i, 0, 0)),
            pl.BlockSpec((bb, tt, d), tile),
            pl.BlockSpec((CONV_WIDTH, d), vec),
            pl.BlockSpec((1, d), vec),
            pl.BlockSpec((1, d), vec),
            pl.BlockSpec((1, d), vec),
        ],
        out_specs=pl.BlockSpec((bb, tt, d), tile),
        scratch_shapes=[pltpu.VMEM((bb, full_rows, d), F32), pltpu.VMEM((bb, tt, d), F32)],
        compiler_params=_params("parallel", "arbitrary"),
        name="conv_ln",
    )(buf, u, wdw, bdw.reshape(1, d), ln_g.reshape(1, d), ln_b.reshape(1, d))


def _pos_term_kernel(pe_ref, w_ref, o_ref):
    o_ref[0] = _dot(pe_ref[0], w_ref[0])


def pos_term(pe, w1):
    n = pe.shape[0]
    k = CMP_LEN * HEAD_DIM
    pe8 = jnp.broadcast_to(pe.reshape(n, 1, k), (n, 8, k)).astype(BF)
    return pl.pallas_call(
        _pos_term_kernel,
        out_shape=jax.ShapeDtypeStruct((n, 8, CMP_HIDDEN), F32),
        grid=(n,),
        in_specs=[pl.BlockSpec((1, 8, k), lambda i: (i, 0, 0)),
                  pl.BlockSpec((1, k, CMP_HIDDEN), lambda i: (i, 0, 0))],
        out_specs=pl.BlockSpec((1, 8, CMP_HIDDEN), lambda i: (i, 0, 0)),
        compiler_params=_params("parallel"),
        name="pos_term",
    )(pe8, w1.reshape(n, k, CMP_HIDDEN).astype(BF))


def _chunk_matrix(x, g):
    cols = [x[:, s * KV_COLS + g * HEAD_DIM: s * KV_COLS + (g + 1) * HEAD_DIM] for s in range(CMP_STRIDE)]
    return jnp.concatenate(cols, axis=1)


TILES_PER_CHUNK = CMP_STRIDE * KV_HEADS // TILE_ROWS
assert 2 * KV_HEADS == TILE_ROWS


def _page_chunk_rows(tile_ref):
    n_chunks = tile_ref.shape[1] // TILES_PER_CHUNK
    low = lax.broadcasted_iota(jnp.int32, (TILE_ROWS, HEAD_DIM), 0) < KV_HEADS
    cols = []
    for s in range(CMP_STRIDE):
        pieces = []
        for c in range(0, n_chunks, 2):
            a = tile_ref[0, TILES_PER_CHUNK * c + s // 2]
            b = tile_ref[0, TILES_PER_CHUNK * (c + 1) + s // 2]
            if s % 2 == 0:
                pieces.append(jnp.where(low, a, pltpu.roll(b, KV_HEADS, axis=0)))
            else:
                pieces.append(jnp.where(low, pltpu.roll(a, KV_HEADS, axis=0), b))
        cols.append(jnp.concatenate(pieces, axis=0))
    return jnp.concatenate(cols, axis=1)


def _compress_mlp(x, shift, w1_ref, pt_ref, w2_ref, b2_ref):
    n = x.shape[0]
    fs = _dot(x, w1_ref[0])
    first = fs[:, :CMP_HIDDEN]
    second = pltpu.roll(fs[:, CMP_HIDDEN:], n - shift, axis=0)
    pre = first + second + pt_ref[0, 0:1, :]
    hid = (pre * _sigmoid(pre)).astype(BF)
    out = _dot(hid, w2_ref[0]) + b2_ref[0]
    row = lax.broadcasted_iota(jnp.int32, (n, 1), 0)
    return jnp.where(row < n - shift, out, 0.0)


def _compress_rows_kernel(k_ref, v_ref, w1_ref, pt_ref, w2_ref, b2_ref, ok_ref, ov_ref):
    for i, (x_ref, o_ref) in enumerate(((k_ref, ok_ref), (v_ref, ov_ref))):
        w = (w1_ref.at[i:i + 1], pt_ref.at[i:i + 1], w2_ref.at[i:i + 1], b2_ref.at[i:i + 1])
        x = x_ref[0]
        for g in range(KV_HEADS):
            out = _compress_mlp(_chunk_matrix(x, g), 1, *w)
            o_ref[0, :, g * HEAD_DIM:(g + 1) * HEAD_DIM] = out.astype(o_ref.dtype)


def _compress_pages_kernel(*refs, n_pages):
    refs = refs[1:]
    w1_ref, pt_ref, w2_ref, b2_ref, ok_ref, ov_ref = refs[2 * n_pages:]
    for i, o_ref in enumerate((ok_ref, ov_ref)):
        w = (w1_ref.at[i:i + 1], pt_ref.at[i:i + 1], w2_ref.at[i:i + 1], b2_ref.at[i:i + 1])
        pages = refs[i * n_pages:(i + 1) * n_pages]
        x = jnp.concatenate([_page_chunk_rows(p).astype(BF) for p in pages], axis=0)
        o_ref[0] = _compress_mlp(x, KV_HEADS, *w).astype(o_ref.dtype)


def _compress_weight_specs():
    fixed3 = lambda *a: (0, 0, 0)
    return [
        pl.BlockSpec((2, CMP_STRIDE * HEAD_DIM, 2 * CMP_HIDDEN), fixed3),
        pl.BlockSpec((2, 8, CMP_HIDDEN), fixed3),
        pl.BlockSpec((2, CMP_HIDDEN, HEAD_DIM), fixed3),
        pl.BlockSpec((2, 1, HEAD_DIM), fixed3),
    ]


def compress_rows(k_rows, v_rows, cw):
    b, n, width = k_rows.shape
    spec = pl.BlockSpec((1, n, width), lambda i: (i, 0, 0))
    out = jax.ShapeDtypeStruct((b, n, KV_COLS), BF)
    ospec = pl.BlockSpec((1, n, KV_COLS), lambda i: (i, 0, 0))
    return pl.pallas_call(
        _compress_rows_kernel,
        out_shape=(out, out),
        grid=(b,),
        in_specs=[spec, spec] + _compress_weight_specs(),
        out_specs=(ospec, ospec),
        compiler_params=_params("parallel"),
        name="compress_rows",
    )(k_rows, v_rows, *cw)


def compress_pages(page_table, n_pages, cache_k, cache_v, cw):
    n_pool, tiles, rows, hd = cache_k.shape
    b = page_table.shape[0] // n_pages
    n = n_pages * tiles // TILES_PER_CHUNK * KV_HEADS

    def page_spec(p):
        return pl.BlockSpec((1, tiles, rows, hd), lambda i, pt: (pt[i * n_pages + p], 0, 0, 0))

    pages = [page_spec(p) for p in range(n_pages)]
    out = jax.ShapeDtypeStruct((b, n, HEAD_DIM), BF)
    ospec = pl.BlockSpec((1, n, HEAD_DIM), lambda i, pt: (i, 0, 0))
    grid_spec = pltpu.PrefetchScalarGridSpec(
        num_scalar_prefetch=1,
        grid=(b,),
        in_specs=pages + pages + _compress_weight_specs(),
        out_specs=(ospec, ospec),
    )
    return pl.pallas_call(
        functools.partial(_compress_pages_kernel, n_pages=n_pages),
        out_shape=(out, out),
        grid_spec=grid_spec,
        compiler_params=_params("parallel"),
        name="compress_pages",
    )(page_table, *([cache_k] * n_pages), *([cache_v] * n_pages), *cw)


def _softmax_cols(s, mask):
    s = jnp.where(mask, s, NEG_INF)
    m = jnp.max(s, axis=0, keepdims=True)
    m = jnp.where(m == NEG_INF, 0.0, m)
    p = jnp.where(mask, jnp.exp(s - m), 0.0)
    return p / jnp.maximum(jnp.sum(p, axis=0, keepdims=True), 1e-30)


def _top_rank_lanes(score, valid, n):
    idx = lax.broadcasted_iota(jnp.int32, score.shape, 1)
    cnt = jnp.zeros(score.shape, F32)
    for j in range(n):
        other = score[:, j:j + 1]
        tie = jnp.where(idx > j, 1.0, 0.0)
        cnt = cnt + jnp.where(other > score, 1.0, jnp.where(other == score, tie, 0.0))
    return jnp.where(cnt < TOP_N, jnp.where(valid, 1.0, 0.0), 0.0)


def _top_rank_sublanes(score, valid):
    n = score.shape[0]
    groups = [score[v:v + 8, :] for v in range(0, n, 8)]
    cnts = [jnp.zeros(g.shape, F32) for g in groups]
    for j in range(n):
        other = score[j:j + 1, :]
        for v, g in enumerate(groups):
            if 8 * v > j:
                beats = other >= g
            elif 8 * v + 7 <= j:
                beats = other > g
            else:
                sub = 8 * v + lax.broadcasted_iota(jnp.int32, g.shape, 0)
                beats = (other > g) | ((other == g) & (sub > j))
            cnts[v] = cnts[v] + jnp.where(beats, 1.0, 0.0)
    cnt = jnp.concatenate(cnts, axis=0)
    return jnp.where(cnt < TOP_N, jnp.where(valid, 1.0, 0.0), 0.0)


def _flash_update(s, vt, carry):
    m, l, acc = carry
    m_new = jnp.maximum(m, jnp.max(s, axis=0, keepdims=True))
    m_safe = jnp.where(m_new == NEG_INF, 0.0, m_new)
    alpha = jnp.exp(m - m_safe)
    p = jnp.exp(s - m_safe)
    l = alpha * l + jnp.sum(p, axis=0, keepdims=True)
    acc = alpha * acc + _dot(vt, p.astype(BF))
    return m_new, l, acc


def _prompt_attn_kernel(q_ref, gt_ref, kc_ref, vct_ref, ks_ref, vst_ref, kw_ref, vwt_ref,
                        o_ref, sel_ref, *, n_cmp):
    qt = pl.program_id(2)
    n_rows = GROUP * Q_TILE
    q = q_ref[0]
    q_rows = jnp.concatenate([q[:, r * HEAD_DIM:(r + 1) * HEAD_DIM] for r in range(GROUP)], axis=0)
    t0 = qt * Q_TILE
    pos_t = t0 + lax.broadcasted_iota(jnp.int32, (1, Q_TILE), 1)
    pos_rows = jnp.concatenate([pos_t] * GROUP, axis=1)

    n_cpad = kc_ref.shape[1]
    ci = lax.broadcasted_iota(jnp.int32, (n_cpad, 1), 0)
    cmask = (ci * CMP_STRIDE + (CMP_LEN - 1) <= pos_rows) & (ci < n_cmp)
    p_cmp = _softmax_cols(_dot_nt(kc_ref[0], q_rows), cmask)
    o_cmp = _dot(vct_ref[0, 0], p_cmp.astype(BF))

    n_steps = ks_ref.shape[2]
    blk_per_step = KEY_STEP // SEL_BLOCK
    n_sel = n_steps * blk_per_step
    p_sum = p_cmp[:, 0:Q_TILE]
    for r in range(1, GROUP):
        p_sum = p_sum + p_cmp[:, r * Q_TILE:(r + 1) * Q_TILE]
    bj = lax.broadcasted_iota(jnp.int32, (n_sel, n_cpad), 0) * SEL_BLOCK
    bi = lax.broadcasted_iota(jnp.int32, (n_sel, n_cpad), 1) * CMP_STRIDE
    overlap_t = jnp.where((bi < bj + SEL_BLOCK) & (bi + CMP_LEN > bj), 1.0, 0.0).astype(BF)
    hi, mid, lo = _split3(p_sum)
    imp = _dot(overlap_t, hi) + _dot(overlap_t, mid) + _dot(overlap_t, lo)
    blk = lax.broadcasted_iota(jnp.int32, (n_sel, 1), 0)
    cur = pos_t >> SEL_SHIFT
    forced = (blk == 0) | (blk == cur) | (blk == cur - 1)
    causal = blk * SEL_BLOCK <= pos_t
    score = jnp.where(causal, imp + FORCE_BONUS * jnp.where(forced, 1.0, 0.0), NEG_INF)
    sel = _top_rank_sublanes(score, causal)
    for st in range(n_steps):
        sel_ref[st] = sel[st * blk_per_step:(st + 1) * blk_per_step, :]

    init = (jnp.full((1, n_rows), NEG_INF, F32), jnp.zeros((1, n_rows), F32),
            jnp.zeros((HEAD_DIM, n_rows), F32))

    key_in_blk = lax.broadcasted_iota(jnp.int32, (SEL_BLOCK, 1), 0)

    def sel_scores(st):
        picked = sel_ref[st]
        pieces = []
        for h in range(blk_per_step):
            kpos = st * KEY_STEP + h * SEL_BLOCK + key_in_blk
            ok = (kpos <= pos_t) & (picked[h:h + 1, :] > 0.5)
            pieces.append(jnp.where(ok, 0.0, NEG_INF))
        b = jnp.concatenate(pieces, axis=0)
        return _dot_nt(ks_ref[0, 0, st], q_rows) + jnp.concatenate([b] * GROUP, axis=1)

    def sel_step(st, carry):
        return _flash_update(sel_scores(st), vst_ref[0, 0, st], carry)

    n_needed = (t0 + Q_TILE + KEY_STEP - 1) // KEY_STEP
    _, l_sel, acc_sel = lax.fori_loop(0, n_needed, sel_step, init)
    o_sel = acc_sel / jnp.maximum(l_sel, 1e-30)

    n_wt = (WINDOW + Q_TILE) // KEY_TILE
    first = qt - (n_wt - 1)
    tiles = [jnp.maximum(first + i, 0) for i in range(n_wt)]
    k_win = jnp.concatenate([kw_ref[0, 0, kt] for kt in tiles], axis=0)
    vt_win = jnp.concatenate([vwt_ref[0, 0, kt] for kt in tiles], axis=1)
    kpos = first * KEY_TILE + lax.broadcasted_iota(jnp.int32, (n_wt * KEY_TILE, 1), 0)
    ok = (kpos >= 0) & (kpos <= pos_rows) & (pos_rows - kpos < WINDOW)
    s_win = _dot_nt(k_win, q_rows) + jnp.where(ok, 0.0, NEG_INF)
    _, l_win, acc_win = _flash_update(s_win, vt_win, init)
    o_win = acc_win / jnp.maximum(l_win, 1e-30)

    gt = gt_ref[0]
    for r in range(GROUP):
        cols = slice(r * Q_TILE, (r + 1) * Q_TILE)
        base = r * N_BRANCH
        mix = (gt[base:base + 1, :] * o_cmp[:, cols] + gt[base + 1:base + 2, :] * o_sel[:, cols]
               + gt[base + 2:base + 3, :] * o_win[:, cols])
        o_ref[0, :, r * HEAD_DIM:(r + 1) * HEAD_DIM] = mix.T.astype(o_ref.dtype)


def prompt_attention(q, gates_t, kc, vct, ks, vst, kw, vwt, n_cmp):
    b, t, _ = q.shape
    n_q = t // Q_TILE
    n_steps = ks.shape[2]
    n_kt = kw.shape[2]
    n_cpad = kc.shape[1]
    whole = lambda i, g, j: (i, g, 0, 0, 0)
    width = GROUP * HEAD_DIM
    return pl.pallas_call(
        functools.partial(_prompt_attn_kernel, n_cmp=n_cmp),
        out_shape=jax.ShapeDtypeStruct((b, t, N_HEADS * HEAD_DIM), BF),
        grid=(b, KV_HEADS, n_q),
        in_specs=[
            pl.BlockSpec((1, Q_TILE, width), lambda i, g, j: (i, j, g)),
            pl.BlockSpec((1, GROUP * N_BRANCH, Q_TILE), lambda i, g, j: (g, 0, i * n_q + j)),
            pl.BlockSpec((1, n_cpad, HEAD_DIM), lambda i, g, j: (i, 0, g)),
            pl.BlockSpec((1, 1, HEAD_DIM, n_cpad), lambda i, g, j: (i, g, 0, 0)),
            pl.BlockSpec((1, 1, n_steps, KEY_STEP, HEAD_DIM), whole),
            pl.BlockSpec((1, 1, n_steps, HEAD_DIM, KEY_STEP), whole),
            pl.BlockSpec((1, 1, n_kt, KEY_TILE, HEAD_DIM), whole),
            pl.BlockSpec((1, 1, n_kt, HEAD_DIM, KEY_TILE), whole),
        ],
        out_specs=pl.BlockSpec((1, Q_TILE, width), lambda i, g, j: (i, j, g)),
        scratch_shapes=[pltpu.VMEM((n_steps, KEY_STEP // SEL_BLOCK, Q_TILE), F32)],
        compiler_params=_params("parallel", "parallel", "arbitrary"),
        name="prompt_attention",
    )(q, gates_t, kc, vct, ks, vst, kw, vwt)


def _softmax_two(s_a, bias_a, s_b, bias_b):
    s_a = s_a + bias_a
    s_b = s_b + bias_b
    m = jnp.maximum(jnp.max(s_a, axis=1, keepdims=True), jnp.max(s_b, axis=1, keepdims=True))
    m = jnp.where(m == NEG_INF, 0.0, m)
    p_a = jnp.exp(s_a - m)
    p_b = jnp.exp(s_b - m)
    total = jnp.sum(p_a, axis=1, keepdims=True) + jnp.sum(p_b, axis=1, keepdims=True)
    return p_a, p_b, 1.0 / jnp.maximum(total, 1e-30)


def _sample_attn_kernel(*refs, n_pages, t_new, past_len, n_cmp):
    refs = refs[1:]
    q_ref, gate_ref, kc_ref, vc_ref = refs[:4]
    ks_pages = refs[4:4 + n_pages]
    vs_pages = refs[4 + n_pages:4 + 2 * n_pages]
    (ksn_ref, vsn_ref, kwp_ref, vwp_ref, kwn_ref, vwn_ref, o_ref,
     expand_ref, past_bias_ref, win_bias_ref) = refs[4 + 2 * n_pages:]

    n_rows = q_ref.shape[1]
    rows_per_head = n_rows // KV_HEADS
    q = q_ref[0]
    row = lax.broadcasted_iota(jnp.int32, (n_rows, 1), 0)
    row_head = _div(row, rows_per_head)
    pos = past_len + _mod(row, t_new)
    n_past = n_pages * ks_pages[0].shape[1]
    n_win = kwp_ref.shape[1]

    def col_key(n_cols):
        return _div(lax.broadcasted_iota(jnp.int32, (1, n_cols), 1), KV_HEADS)

    def head_bias(n_cols):
        col = lax.broadcasted_iota(jnp.int32, (1, n_cols), 1)
        return jnp.where(_mod(col, KV_HEADS) == row_head, 0.0, NEG_INF)

    @pl.when(pl.program_id(0) == 0)
    def _():
        ej = lax.broadcasted_iota(jnp.int32, (128, n_past), 0)
        ek = _div(lax.broadcasted_iota(jnp.int32, (128, n_past), 1), KV_HEADS)
        expand_ref[...] = jnp.where((ek >> SEL_SHIFT) == ej, 1.0, 0.0).astype(BF)
        past_bias_ref[...] = jnp.where(col_key(n_past) <= pos, head_bias(n_past), NEG_INF)
        wpos = past_len - n_win // KV_HEADS + col_key(n_win)
        win_bias_ref[...] = jnp.where((wpos <= pos) & (pos - wpos < WINDOW), head_bias(n_win), NEG_INF)

    new_pos = past_len + col_key(128)
    new_bias = jnp.where((col_key(128) < t_new) & (new_pos <= pos), head_bias(128), NEG_INF)

    def pad_new(ref):
        x = ref[0]
        return jnp.concatenate([x, jnp.zeros((128 - x.shape[0], x.shape[1]), x.dtype)], axis=0).astype(BF)

    n_c = kc_ref.shape[1]
    chunk = col_key(n_c)
    c_ok = (chunk * CMP_STRIDE + (CMP_LEN - 1) <= pos) & (chunk < n_cmp)
    s_c = _dot_nt(q, kc_ref[0]) + jnp.where(c_ok, head_bias(n_c), NEG_INF)
    m_c = jnp.max(s_c, axis=1, keepdims=True)
    m_c = jnp.where(m_c == NEG_INF, 0.0, m_c)
    p_c = jnp.exp(s_c - m_c)
    p_c = p_c / jnp.maximum(jnp.sum(p_c, axis=1, keepdims=True), 1e-30)
    o_cmp = _dot(p_c.astype(BF), vc_ref[0])

    ra = lax.broadcasted_iota(jnp.int32, (n_rows, n_rows), 0)
    rb = lax.broadcasted_iota(jnp.int32, (n_rows, n_rows), 1)
    same_gt = ((_div(ra, rows_per_head) == _div(rb, rows_per_head))
               & (_mod(ra, t_new) == _mod(rb, t_new)))
    group_sum = jnp.where(same_gt, 1.0, 0.0).astype(BF)
    hi, mid, lo = _split3(p_c)
    p_sum = _dot(group_sum, hi) + _dot(group_sum, mid) + _dot(group_sum, lo)
    n_keys = past_len + t_new
    n_sel = -(-n_keys // SEL_BLOCK)
    bi = _div(lax.broadcasted_iota(jnp.int32, (n_c, 128), 0), KV_HEADS) * CMP_STRIDE
    bj = lax.broadcasted_iota(jnp.int32, (n_c, 128), 1) * SEL_BLOCK
    overlap = jnp.where((bi < bj + SEL_BLOCK) & (bi + CMP_LEN > bj), 1.0, 0.0).astype(BF)
    imp = _dot01(p_sum, overlap)
    blk = lax.broadcasted_iota(jnp.int32, (1, 128), 1)
    cur = pos >> SEL_SHIFT
    forced = (blk == 0) | (blk == cur) | (blk == cur - 1)
    causal = (blk * SEL_BLOCK <= pos) & (blk < n_sel)
    score = jnp.where(causal, imp + FORCE_BONUS * jnp.where(forced, 1.0, 0.0), NEG_INF)
    sel = _top_rank_lanes(score, causal, n_sel)

    k_past = jnp.concatenate([p[0].astype(BF) for p in ks_pages], axis=0)
    v_past = jnp.concatenate([p[0].astype(BF) for p in vs_pages], axis=0)
    sel16 = sel.astype(BF)
    bias_past = jnp.where(_dot(sel16, expand_ref[...]) > 0.5, past_bias_ref[...], NEG_INF)
    nj = lax.broadcasted_iota(jnp.int32, (128, 128), 0)
    nk = past_len + _div(lax.broadcasted_iota(jnp.int32, (128, 128), 1), KV_HEADS)
    expand_new = jnp.where((nk >> SEL_SHIFT) == nj, 1.0, 0.0).astype(BF)
    bias_new = jnp.where(_dot(sel16, expand_new) > 0.5, new_bias, NEG_INF)
    p_a, p_b, inv = _softmax_two(_dot_nt(q, k_past), bias_past, _dot_nt(q, pad_new(ksn_ref)), bias_new)
    o_sel = (_dot(p_a.astype(BF), v_past) + _dot(p_b.astype(BF), pad_new(vsn_ref))) * inv

    bias_wn = jnp.where(pos - new_pos < WINDOW, new_bias, NEG_INF)
    p_a, p_b, inv = _softmax_two(_dot_nt(q, kwp_ref[0].astype(BF)), win_bias_ref[...],
                                 _dot_nt(q, pad_new(kwn_ref)), bias_wn)
    o_win = (_dot(p_a.astype(BF), vwp_ref[0].astype(BF)) + _dot(p_b.astype(BF), pad_new(vwn_ref))) * inv

    gate = gate_ref[0]
    o = gate[:, 0:1] * o_cmp + gate[:, 1:2] * o_sel + gate[:, 2:3] * o_win
    o_ref[0] = o.astype(o_ref.dtype)


def sample_attention(page_table, q, gates, kc, vc, cache_ks, cache_vs, ks_new, vs_new,
                     kw_past, vw_past, kw_new, vw_new, *, t_new, n_cmp):
    b, n_rows, _ = q.shape
    n_pages = page_table.shape[0] // b
    page_rows = cache_ks.shape[1]
    n_c = kc.shape[1]
    n_win = kw_past.shape[1]
    n_new = ks_new.shape[1]
    n_past = n_pages * page_rows

    def page_spec(p):
        return pl.BlockSpec((1, page_rows, HEAD_DIM), lambda i, pt: (pt[i * n_pages + p], 0, 0))

    per_seq = lambda rows, cols=HEAD_DIM: pl.BlockSpec((1, rows, cols), lambda i, pt: (i, 0, 0))
    pages = [page_spec(p) for p in range(n_pages)]
    grid_spec = pltpu.PrefetchScalarGridSpec(
        num_scalar_prefetch=1,
        grid=(b,),
        in_specs=[per_seq(n_rows), per_seq(n_rows, N_BRANCH), per_seq(n_c), per_seq(n_c)]
        + pages + pages
        + [per_seq(n_new), per_seq(n_new), per_seq(n_win), per_seq(n_win), per_seq(n_new), per_seq(n_new)],
        out_specs=per_seq(n_rows),
        scratch_shapes=[pltpu.VMEM((128, n_past), BF), pltpu.VMEM((n_rows, n_past), F32),
                        pltpu.VMEM((n_rows, n_win), F32)],
    )
    kern = functools.partial(_sample_attn_kernel, n_pages=n_pages, t_new=t_new,
                             past_len=n_past // KV_HEADS, n_cmp=n_cmp)
    return pl.pallas_call(
        kern,
        out_shape=jax.ShapeDtypeStruct((b, n_rows, HEAD_DIM), BF),
        grid_spec=grid_spec,
        compiler_params=_params("arbitrary"),
        name="sample_attention",
    )(page_table, q, gates, kc, vc, *([cache_ks] * n_pages), *([cache_vs] * n_pages),
      ks_new, vs_new, kw_past, vw_past, kw_new, vw_new)


def _window_update_kernel(k_ref, v_ref, kn_ref, vn_ref, ok_ref, ov_ref):
    n = k_ref.shape[1]
    n_new = kn_ref.shape[1]
    for s_ref, n_ref, o_ref in ((k_ref, kn_ref, ok_ref), (v_ref, vn_ref, ov_ref)):
        o_ref[:, 0:n - n_new, :] = s_ref[:, n_new:n, :]
        o_ref[:, n - n_new:n, :] = n_ref[...]


def window_update(k_state, v_state, k_new, v_new):
    b, n, d = k_state.shape
    n_new = k_new.shape[1]
    bb = 2
    state = pl.BlockSpec((bb, n, d), lambda i: (i, 0, 0))
    new = pl.BlockSpec((bb, n_new, d), lambda i: (i, 0, 0))
    out = jax.ShapeDtypeStruct((b, n, d), F32)
    return pl.pallas_call(
        _window_update_kernel,
        out_shape=(out, out),
        grid=(b // bb,),
        in_specs=[state, state, new, new],
        out_specs=(state, state),
        compiler_params=_params("parallel"),
        name="window_update",
    )(k_state, v_state, k_new, v_new)


def _rope_tables(pos):
    half = HEAD_DIM // 2
    inv = ROPE_THETA ** (-jnp.arange(half, dtype=F32) / half)
    ang = pos.astype(F32)[:, None] * inv[None, :]
    cos = jnp.cos(ang)
    sin = jnp.sin(ang)
    return jnp.concatenate([cos, cos], axis=1), jnp.concatenate([-sin, sin], axis=1)


def kernel(x_prompt, x_sample, cache_k_cmp, cache_v_cmp, cache_k_sel, cache_v_sel, state_k_win, state_v_win, state_conv, page_table, norm_g, conv_w1, conv_b1, conv_wdw, conv_bdw, conv_ln_g, conv_ln_b, conv_w2, conv_b2, kv_norm_g, w_kv, cmp_w1, cmp_pe, cmp_w2, cmp_b2, nsa_w_qg, nsa_w_o, ffn_w_in, ffn_w_out):
    bp, tp, d = x_prompt.shape
    bs, ts, _ = x_sample.shape
    n_p = bp * tp
    n_s = bs * ts
    n_pool, page, _, _ = cache_k_sel.shape
    n_pages = page_table.shape[1]
    past_len = n_pages * page
    bf = lambda w: w.astype(BF)

    def split(a):
        return a[:n_p], a[n_p:]

    x_in = (x_prompt.reshape(n_p, d), x_sample.reshape(n_s, d))
    pos = jnp.concatenate([jnp.tile(jnp.arange(tp, dtype=jnp.int32), bp),
                           jnp.tile(past_len + jnp.arange(ts, dtype=jnp.int32), bs)])
    cos, sin = _rope_tables(pos)

    g = norm_g[0]
    u = glu_in(x_in, g[0], bf(conv_w1[0]), conv_b1[0])
    conv_tile = 128
    assert tp % conv_tile == 0 and (n_p + n_s) % conv_tile == 0
    u_s = u[n_p:].reshape(bs, ts, d)
    conv_args = (conv_wdw[0], conv_bdw[0], conv_ln_g[0], conv_ln_b[0])
    v_p = conv_ln(jnp.zeros((bp, CONV_HALO, d), F32), u.reshape(-1, conv_tile, d), *conv_args,
                  bb=1, tt=conv_tile, n_t=tp // conv_tile)
    v_s = conv_ln(state_conv[0], u_s, *conv_args, bb=16, tt=ts)
    x = mm_norm_res((v_p.reshape(n_p, d), v_s.reshape(n_s, d)), bf(conv_w2[0]), conv_b2[0], g[1], x_in)
    ffn_in16, ffn_out16 = bf(ffn_w_in), bf(ffn_w_out)
    x = ffn(x, g[2], ffn_in16, ffn_out16, g[3], 0)
    conv_p = u[:n_p].reshape(bp, tp, d)[:, tp - CONV_HALO:][None]
    conv_s = jnp.concatenate([state_conv[0][:, ts:], u_s], axis=1)[None]

    kv32, kv16 = kv_proj(x, kv_norm_g, bf(w_kv), cos, sin)
    rows_p = [kv32[s, :n_p].reshape(bp, tp, KV_HEADS, HEAD_DIM) for s in range(N_KV_SLOTS)]
    rows_s = [kv32[s, n_p:].reshape(bs, ts, KV_HEADS, HEAD_DIM) for s in range(N_KV_SLOTS)]
    n_wp = min(WINDOW, tp)
    kw_p, vw_p = rows_p[4][:, tp - n_wp:], rows_p[5][:, tp - n_wp:]

    def new_rows(slot):
        return kv32[slot, n_p:].reshape(bs, ts * KV_HEADS, HEAD_DIM)

    head_rows = lambda a: a.reshape(a.shape[0], -1, HEAD_DIM)
    kw_s, vw_s = window_update(head_rows(state_k_win), head_rows(state_v_win), new_rows(4), new_rows(5))
    kw_s, vw_s = kw_s.reshape(state_k_win.shape), vw_s.reshape(state_v_win.shape)

    w1 = cmp_w1.reshape(2, 2, CMP_STRIDE * HEAD_DIM, CMP_HIDDEN)
    w1cat = bf(jnp.concatenate([w1[:, 0], w1[:, 1]], axis=-1))
    cw = (w1cat, pos_term(cmp_pe, cmp_w1), bf(cmp_w2), cmp_b2.reshape(2, 1, HEAD_DIM))
    chunk_cols = CMP_STRIDE * KV_COLS
    kc_p, vc_p = compress_rows(kv16[0, :n_p].reshape(bp, tp // CMP_STRIDE, chunk_cols),
                               kv16[1, :n_p].reshape(bp, tp // CMP_STRIDE, chunk_cols), cw)
    pt_flat = page_table.reshape(-1)
    assert (past_len + ts) // CMP_STRIDE == past_len // CMP_STRIDE
    page_rows = page * KV_HEADS
    as_tiles = lambda c: c.reshape(n_pool, page_rows // TILE_ROWS, TILE_ROWS, HEAD_DIM)
    kc_s, vc_s = compress_pages(pt_flat, n_pages, as_tiles(cache_k_cmp), as_tiles(cache_v_cmp), cw)

    g = norm_g[1]
    w_qg = nsa_w_qg[0]
    n_q_cols = N_HEADS * HEAD_DIM
    n_gate = N_HEADS * N_BRANCH
    wg = w_qg[:, n_q_cols:]
    wg_pad = bf(jnp.pad(wg, ((0, 0), (0, 128 - n_gate))))
    q, gates, gates_t = q_proj(x, g[0], bf(w_qg[:, :n_q_cols]), wg_pad, bf(wg.T), cos, sin)

    assert tp % KEY_STEP == 0 and tp % Q_TILE == 0

    def key_tiles(a, tk):
        return a.reshape(bp, tp // tk, tk, KV_HEADS, HEAD_DIM).transpose(0, 3, 1, 2, 4)

    def value_tiles_t(a, tk):
        return a.reshape(bp, tp // tk, tk, KV_HEADS, HEAD_DIM).transpose(0, 3, 1, 4, 2)

    n_cmp_p = tp // CMP_STRIDE - 1
    vct_p = vc_p.reshape(bp, tp // CMP_STRIDE, KV_HEADS, HEAD_DIM).transpose(0, 2, 3, 1)
    gates_t_p = gates_t[:, :n_p].reshape(KV_HEADS, GROUP * N_BRANCH, n_p)
    o_p = prompt_attention(q[:n_p].reshape(bp, tp, n_q_cols), gates_t_p, kc_p, vct_p,
                           key_tiles(kv16[2, :n_p], KEY_STEP), value_tiles_t(kv16[3, :n_p], KEY_STEP),
                           key_tiles(kv16[4, :n_p], KEY_TILE), value_tiles_t(kv16[5, :n_p], KEY_TILE), n_cmp_p)

    def seq_rows(a, width):
        return a.reshape(bs, ts, KV_HEADS, GROUP, width).transpose(0, 2, 3, 1, 4).reshape(bs, -1, width)

    q_s = seq_rows(q[n_p:], HEAD_DIM)
    gates_s = seq_rows(gates[n_p:, :n_gate], N_BRANCH)

    o_s = sample_attention(
        pt_flat, q_s, gates_s, kc_s, vc_s,
        head_rows(cache_k_sel), head_rows(cache_v_sel), new_rows(2), new_rows(3),
        head_rows(state_k_win), head_rows(state_v_win), new_rows(4), new_rows(5), t_new=ts, n_cmp=past_len // CMP_STRIDE - 1)
    o_s = o_s.reshape(bs, KV_HEADS, GROUP, ts, HEAD_DIM).transpose(0, 3, 1, 2, 4).reshape(n_s, n_q_cols)
    x = mm_norm_res((o_p.reshape(n_p, n_q_cols), o_s), bf(nsa_w_o[0]), None, g[1], x)
    x = ffn(x, g[2], ffn_in16, ffn_out16, g[3], 1)

    y_p, y_s = split(x)
    return (y_p.reshape(bp, tp, d), y_s.reshape(bs, ts, d), conv_p,
            rows_p[0], rows_p[1], rows_p[2], rows_p[3], kw_p, vw_p,
            conv_s, rows_s[0], rows_s[1], rows_s[2], rows_s[3], kw_s, vw_s)
```

```python
import functools

import jax
import jax.numpy as jnp
from jax import lax
from jax.experimental import pallas as pl
from jax.experimental.pallas import tpu as pltpu

D_MODEL = 2048
CONV_WIDTH = 31
CONV_HALO = CONV_WIDTH - 1
N_HEADS = 16
HEAD_DIM = 128
KV_HEADS = 4
GROUP = N_HEADS // KV_HEADS
N_BRANCH = 3
N_KV_SLOTS = 6
KV_COLS = KV_HEADS * HEAD_DIM
CMP_LEN = 32
CMP_STRIDE = 16
CMP_HIDDEN = 256
SEL_BLOCK = 64
TOP_N = 16
WINDOW = 512
ROPE_THETA = 10000.0
FORCE_BONUS = 1.0e4
EPS = 1e-6
SCALE = HEAD_DIM ** -0.5
SEL_SHIFT = SEL_BLOCK.bit_length() - 1
assert 1 << SEL_SHIFT == SEL_BLOCK

TILE_ROWS = 8
BF = jnp.bfloat16
F32 = jnp.float32
NEG_INF = float("-inf")

VMEM_LIMIT_BYTES = 56 * 1024 * 1024
ROW_TILE = 512
KEY_TILE = 128
KEY_STEP = 1024
Q_TILE = 256


def _params(*sem):
    return pltpu.CompilerParams(dimension_semantics=sem, vmem_limit_bytes=VMEM_LIMIT_BYTES)


def _sigmoid(x):
    return 1.0 / (1.0 + jnp.exp(-x))


def _rms(x, g):
    return x * lax.rsqrt(jnp.mean(x * x, axis=-1, keepdims=True) + EPS) * g


def _rope(y, cos, sin_signed):
    return y * cos + pltpu.roll(y, HEAD_DIM // 2, axis=1) * sin_signed


def _split3(x):
    hi = x.astype(BF)
    r1 = x - hi.astype(F32)
    mid = r1.astype(BF)
    lo = (r1 - mid.astype(F32)).astype(BF)
    return hi, mid, lo


def _div(x, n):
    assert n & (n - 1) == 0
    return x >> (n.bit_length() - 1)


def _mod(x, n):
    assert n & (n - 1) == 0
    return x & (n - 1)


def _dot(a, b):
    return jnp.dot(a, b, preferred_element_type=F32)


def _dot_nt(a, b):
    return lax.dot_general(a, b, (((1,), (1,)), ((), ())), preferred_element_type=F32)


def _dot01(x, onehot):
    hi, mid, lo = _split3(x)
    return _dot(hi, onehot) + _dot(mid, onehot) + _dot(lo, onehot)


def _row_pair_specs(n_first, block):
    first = pl.BlockSpec(block, lambda i, *_: (jnp.minimum(i, n_first - 1), 0))
    second = pl.BlockSpec(block, lambda i, *_: (jnp.maximum(i - n_first, 0), 0))
    return [first, second]


def _glu_in_kernel(xa_ref, xb_ref, g_ref, wa_ref, wb_ref, ba_ref, bb_ref, o_ref, xn_ref, *, n_first):
    i = pl.program_id(0)
    j = pl.program_id(1)

    @pl.when((j == 0) & (i < n_first))
    def _():
        xn_ref[...] = _rms(xa_ref[...], g_ref[...]).astype(BF)

    @pl.when((j == 0) & (i >= n_first))
    def _():
        xn_ref[...] = _rms(xb_ref[...], g_ref[...]).astype(BF)

    xn = xn_ref[...]
    a = _dot(xn, wa_ref[...]) + ba_ref[...]
    b = _dot(xn, wb_ref[...]) + bb_ref[...]
    o_ref[...] = a * _sigmoid(b)


def glu_in(x_pair, g, w, b):
    xa, xb = x_pair
    d = xa.shape[1]
    m = xa.shape[0] + xb.shape[0]
    n_first = xa.shape[0] // ROW_TILE
    tn = 512
    nj = d // tn
    return pl.pallas_call(
        functools.partial(_glu_in_kernel, n_first=n_first),
        out_shape=jax.ShapeDtypeStruct((m, d), F32),
        grid=(m // ROW_TILE, nj),
        in_specs=_row_pair_specs(n_first, (ROW_TILE, d)) + [
            pl.BlockSpec((1, d), lambda i, j: (0, 0)),
            pl.BlockSpec((d, tn), lambda i, j: (0, j)),
            pl.BlockSpec((d, tn), lambda i, j: (0, j + nj)),
            pl.BlockSpec((1, tn), lambda i, j: (0, j)),
            pl.BlockSpec((1, tn), lambda i, j: (0, j + nj)),
        ],
        out_specs=pl.BlockSpec((ROW_TILE, tn), lambda i, j: (i, j)),
        scratch_shapes=[pltpu.VMEM((ROW_TILE, d), BF)],
        compiler_params=_params("parallel", "arbitrary"),
        name="glu_in",
    )(xa, xb, g.reshape(1, d), w, w, b.reshape(1, 2 * d), b.reshape(1, 2 * d))


def _mm_norm_res_kernel(*refs, has_bias, x_is_pair, n_first):
    ha_ref, hb_ref, w_ref = refs[:3]
    refs = refs[3:]
    if has_bias:
        b_ref, refs = refs[0], refs[1:]
    g_ref, refs = refs[0], refs[1:]
    x_refs, (o_ref, h_ref) = refs[:-2], refs[-2:]
    is_first = pl.program_id(0) < n_first

    @pl.when(is_first)
    def _():
        h_ref[...] = ha_ref[...].astype(BF)

    @pl.when(jnp.logical_not(is_first))
    def _():
        h_ref[...] = hb_ref[...].astype(BF)

    y = _dot(h_ref[...], w_ref[...])
    if has_bias:
        y = y + b_ref[...]
    r = _rms(y, g_ref[...])
    if x_is_pair:
        @pl.when(is_first)
        def _():
            o_ref[...] = x_refs[0][...] + r

        @pl.when(jnp.logical_not(is_first))
        def _():
            o_ref[...] = x_refs[1][...] + r
    else:
        o_ref[...] = x_refs[0][...] + r


def mm_norm_res(h_pair, w, b, g, x):
    ha, hb = h_pair
    k = ha.shape[1]
    m = ha.shape[0] + hb.shape[0]
    d = w.shape[1]
    tm = 256
    n_first = ha.shape[0] // tm
    row = lambda i: (i, 0)
    fixed = lambda i: (0, 0)
    in_specs = _row_pair_specs(n_first, (tm, k)) + [pl.BlockSpec((k, d), fixed)]
    args = [ha, hb, w]
    if b is not None:
        in_specs.append(pl.BlockSpec((1, d), fixed))
        args.append(b.reshape(1, d))
    in_specs.append(pl.BlockSpec((1, d), fixed))
    args.append(g.reshape(1, d))
    x_is_pair = isinstance(x, tuple)
    if x_is_pair:
        in_specs += _row_pair_specs(n_first, (tm, d))
        args += list(x)
    else:
        in_specs.append(pl.BlockSpec((tm, d), row))
        args.append(x)
    return pl.pallas_call(
        functools.partial(_mm_norm_res_kernel, has_bias=b is not None, x_is_pair=x_is_pair, n_first=n_first),
        out_shape=jax.ShapeDtypeStruct((m, d), F32),
        grid=(m // tm,),
        in_specs=in_specs,
        out_specs=pl.BlockSpec((tm, d), row),
        scratch_shapes=[pltpu.VMEM((tm, k), BF)],
        compiler_params=_params("parallel"),
        name="mm_norm_res",
    )(*args)


def _ffn_kernel(x_ref, gi_ref, wa_ref, wb_ref, wo_ref, go_ref, o_ref, xn_ref, acc_ref):
    j = pl.program_id(1)

    @pl.when(j == 0)
    def _():
        xn_ref[...] = _rms(x_ref[...], gi_ref[...]).astype(BF)
        acc_ref[...] = jnp.zeros_like(acc_ref)

    xn = xn_ref[...]
    a = _dot(xn, wa_ref[0])
    b = _dot(xn, wb_ref[0])
    h = (a * _sigmoid(a) * b).astype(BF)
    acc_ref[...] += _dot(h, wo_ref[0])

    @pl.when(j == pl.num_programs(1) - 1)
    def _():
        o_ref[...] = x_ref[...] + _rms(acc_ref[...], go_ref[...])


def ffn(x, g_in, w_in, w_out, g_out, layer):
    m, d = x.shape
    f = w_out.shape[1]
    tf = 512
    nj = f // tf
    return pl.pallas_call(
        _ffn_kernel,
        out_shape=jax.ShapeDtypeStruct((m, d), F32),
        grid=(m // ROW_TILE, nj),
        in_specs=[
            pl.BlockSpec((ROW_TILE, d), lambda i, j: (i, 0)),
            pl.BlockSpec((1, d), lambda i, j: (0, 0)),
            pl.BlockSpec((1, d, tf), lambda i, j: (layer, 0, j)),
            pl.BlockSpec((1, d, tf), lambda i, j: (layer, 0, j + nj)),
            pl.BlockSpec((1, tf, d), lambda i, j: (layer, j, 0)),
            pl.BlockSpec((1, d), lambda i, j: (0, 0)),
        ],
        out_specs=pl.BlockSpec((ROW_TILE, d), lambda i, j: (i, 0)),
        scratch_shapes=[pltpu.VMEM((ROW_TILE, d), BF), pltpu.VMEM((ROW_TILE, d), F32)],
        compiler_params=_params("parallel", "arbitrary"),
        name="ffn",
    )(x, g_in.reshape(1, d), w_in, w_in, w_out, g_out.reshape(1, d))


def _kv_proj_kernel(x_ref, g_ref, w_ref, cos_ref, sin_ref, o32_ref, o16_ref, xn_ref):
    s = pl.program_id(1)

    @pl.when(s == 0)
    def _():
        xn_ref[...] = _rms(x_ref[...], g_ref[...]).astype(BF)

    y = _dot(xn_ref[...], w_ref[...])

    @pl.when(s % 2 == 0)
    def _():
        cos = cos_ref[...]
        sin = sin_ref[...]
        for h in range(KV_HEADS):
            cols = slice(h * HEAD_DIM, (h + 1) * HEAD_DIM)
            r = _rope(y[:, cols], cos, sin)
            o32_ref[0, :, h, :] = r
            o16_ref[0, :, cols] = r.astype(BF)

    @pl.when(s % 2 == 1)
    def _():
        for h in range(KV_HEADS):
            o32_ref[0, :, h, :] = y[:, h * HEAD_DIM:(h + 1) * HEAD_DIM]
        o16_ref[0] = y.astype(BF)


def kv_proj(x, g, w, cos, sin):
    m, d = x.shape
    out = jax.ShapeDtypeStruct((N_KV_SLOTS, m, KV_HEADS, HEAD_DIM), F32)
    out16 = jax.ShapeDtypeStruct((N_KV_SLOTS, m, KV_COLS), BF)
    return pl.pallas_call(
        _kv_proj_kernel,
        out_shape=(out, out16),
        grid=(m // ROW_TILE, N_KV_SLOTS),
        in_specs=[
            pl.BlockSpec((ROW_TILE, d), lambda i, s: (i, 0)),
            pl.BlockSpec((1, d), lambda i, s: (0, 0)),
            pl.BlockSpec((d, KV_COLS), lambda i, s: (0, s)),
            pl.BlockSpec((ROW_TILE, HEAD_DIM), lambda i, s: (i, 0)),
            pl.BlockSpec((ROW_TILE, HEAD_DIM), lambda i, s: (i, 0)),
        ],
        out_specs=(pl.BlockSpec((1, ROW_TILE, KV_HEADS, HEAD_DIM), lambda i, s: (s, i, 0, 0)),
                   pl.BlockSpec((1, ROW_TILE, KV_COLS), lambda i, s: (s, i, 0))),
        scratch_shapes=[pltpu.VMEM((ROW_TILE, d), BF)],
        compiler_params=_params("parallel", "arbitrary"),
        name="kv_proj",
    )(x, g.reshape(1, d), w, cos, sin)


def _q_proj_kernel(x_ref, g_ref, wq_ref, wg_ref, wgt_ref, cos_ref, sin_ref,
                   q_ref, gate_ref, gatet_ref, xn_ref):
    @pl.when(pl.program_id(1) == 0)
    def _():
        xn = _rms(x_ref[...], g_ref[...]).astype(BF)
        xn_ref[...] = xn
        gate_ref[...] = _sigmoid(_dot(xn, wg_ref[...]))
        gatet_ref[...] = _sigmoid(_dot_nt(wgt_ref[...], xn))

    y = _dot(xn_ref[...], wq_ref[...])
    cos = cos_ref[...]
    sin = sin_ref[...]
    for h in range(GROUP):
        cols = slice(h * HEAD_DIM, (h + 1) * HEAD_DIM)
        q_ref[:, cols] = (_rope(y[:, cols], cos, sin) * SCALE).astype(BF)


def q_proj(x, g, wq, wg_pad, wg_t, cos, sin):
    m, d = x.shape
    n_gate = wg_t.shape[0]
    tn = GROUP * HEAD_DIM
    return pl.pallas_call(
        _q_proj_kernel,
        out_shape=(jax.ShapeDtypeStruct((m, N_HEADS * HEAD_DIM), BF),
                   jax.ShapeDtypeStruct((m, 128), F32),
                   jax.ShapeDtypeStruct((n_gate, m), F32)),
        grid=(m // ROW_TILE, KV_HEADS),
        in_specs=[
            pl.BlockSpec((ROW_TILE, d), lambda i, j: (i, 0)),
            pl.BlockSpec((1, d), lambda i, j: (0, 0)),
            pl.BlockSpec((d, tn), lambda i, j: (0, j)),
            pl.BlockSpec((d, 128), lambda i, j: (0, 0)),
            pl.BlockSpec((n_gate, d), lambda i, j: (0, 0)),
            pl.BlockSpec((ROW_TILE, HEAD_DIM), lambda i, j: (i, 0)),
            pl.BlockSpec((ROW_TILE, HEAD_DIM), lambda i, j: (i, 0)),
        ],
        out_specs=(pl.BlockSpec((ROW_TILE, tn), lambda i, j: (i, j)),
                   pl.BlockSpec((ROW_TILE, 128), lambda i, j: (i, 0)),
                   pl.BlockSpec((n_gate, ROW_TILE), lambda i, j: (0, i))),
        scratch_shapes=[pltpu.VMEM((ROW_TILE, d), BF)],
        compiler_params=_params("parallel", "arbitrary"),
        name="q_proj",
    )(x, g.reshape(1, d), wq, wg_pad, wg_t, cos, sin)


CONV_PAD = 32


def _conv_kernel(buf_ref, u_ref, w_ref, bdw_ref, lg_ref, lb_ref, o_ref, full_ref, y_ref,
                 *, tt, n_t, col_chunk):
    lo = CONV_PAD - CONV_HALO
    if n_t > 1:
        t = pl.program_id(1)

        @pl.when(t == 0)
        def _():
            full_ref[:, lo:CONV_PAD, :] = buf_ref[...]

        @pl.when(t > 0)
        def _():
            full_ref[:, lo:CONV_PAD, :] = full_ref[:, tt + lo:tt + CONV_PAD, :]
    else:
        full_ref[:, lo:CONV_PAD, :] = buf_ref[...]
    full_ref[:, CONV_PAD:CONV_PAD + tt, :] = u_ref[...]

    d = u_ref.shape[-1]
    if tt % TILE_ROWS == 0:
        full_ref[:, CONV_PAD + tt:, :] = jnp.zeros((full_ref.shape[0], TILE_ROWS, d), F32)
        for c0 in range(0, d, col_chunk):
            cols = slice(c0, c0 + col_chunk)
            acc = None
            for rem in range(TILE_ROWS):
                z = None
                for k in range(CONV_WIDTH):
                    if (lo + k) % TILE_ROWS != rem:
                        continue
                    base = lo + k - rem
                    term = w_ref[k:k + 1, cols] * full_ref[:, base:base + tt + TILE_ROWS, cols]
                    z = term if z is None else z + term
                z = z[:, rem:rem + tt, :]
                acc = z if acc is None else acc + z
            y_ref[:, :, cols] = acc + bdw_ref[:, cols]
    else:
        for c0 in range(0, d, col_chunk):
            cols = slice(c0, c0 + col_chunk)
            acc = None
            for k in range(CONV_WIDTH):
                term = w_ref[k:k + 1, cols] * full_ref[:, lo + k:lo + k + tt, cols]
                acc = term if acc is None else acc + term
            y_ref[:, :, cols] = acc + bdw_ref[:, cols]

    y = y_ref[...]
    mu = jnp.mean(y, axis=-1, keepdims=True)
    yc = y - mu
    z = yc * lax.rsqrt(jnp.mean(yc * yc, axis=-1, keepdims=True) + EPS) * lg_ref[...] + lb_ref[...]
    o_ref[...] = z * _sigmoid(z)


def conv_ln(buf, u, wdw, bdw, ln_g, ln_b, *, bb, tt, n_t=1):
    b, _, d = buf.shape
    aligned = tt % TILE_ROWS == 0
    kern = functools.partial(_conv_kernel, tt=tt, n_t=n_t, col_chunk=128 if aligned else 256)
    full_rows = CONV_PAD + tt + (TILE_ROWS if aligned else 0)
    vec = lambda i, j: (0, 0)
    tile = lambda i, j: (i * n_t + j, 0, 0)
    return pl.pallas_call(
        kern,
        out_shape=jax.ShapeDtypeStruct((b * n_t, tt, d), F32),
        grid=(b // bb, n_t),
        in_specs=[
            pl.BlockSpec((bb, CONV_HALO, d), lambda i, j: (i, 0, 0)),
            pl.BlockSpec((bb, tt, d), tile),
            pl.BlockSpec((CONV_WIDTH, d), vec),
            pl.BlockSpec((1, d), vec),
            pl.BlockSpec((1, d), vec),
            pl.BlockSpec((1, d), vec),
        ],
        out_specs=pl.BlockSpec((bb, tt, d), tile),
        scratch_shapes=[pltpu.VMEM((bb, full_rows, d), F32), pltpu.VMEM((bb, tt, d), F32)],
        compiler_params=_params("parallel", "arbitrary"),
        name="conv_ln",
    )(buf, u, wdw, bdw.reshape(1, d), ln_g.reshape(1, d), ln_b.reshape(1, d))


def _pos_term_kernel(pe_ref, w_ref, o_ref):
    o_ref[0] = _dot(pe_ref[0], w_ref[0])


def pos_term(pe, w1):
    n = pe.shape[0]
    k = CMP_LEN * HEAD_DIM
    pe8 = jnp.broadcast_to(pe.reshape(n, 1, k), (n, 8, k)).astype(BF)
    return pl.pallas_call(
        _pos_term_kernel,
        out_shape=jax.ShapeDtypeStruct((n, 8, CMP_HIDDEN), F32),
        grid=(n,),
        in_specs=[pl.BlockSpec((1, 8, k), lambda i: (i, 0, 0)),
                  pl.BlockSpec((1, k, CMP_HIDDEN), lambda i: (i, 0, 0))],
        out_specs=pl.BlockSpec((1, 8, CMP_HIDDEN), lambda i: (i, 0, 0)),
        compiler_params=_params("parallel"),
        name="pos_term",
    )(pe8, w1.reshape(n, k, CMP_HIDDEN).astype(BF))


def _chunk_matrix(x, g):
    cols = [x[:, s * KV_COLS + g * HEAD_DIM: s * KV_COLS + (g + 1) * HEAD_DIM] for s in range(CMP_STRIDE)]
    return jnp.concatenate(cols, axis=1)


TILES_PER_CHUNK = CMP_STRIDE * KV_HEADS // TILE_ROWS
assert 2 * KV_HEADS == TILE_ROWS


def _page_chunk_rows(tile_ref):
    n_chunks = tile_ref.shape[1] // TILES_PER_CHUNK
    low = lax.broadcasted_iota(jnp.int32, (TILE_ROWS, HEAD_DIM), 0) < KV_HEADS
    cols = []
    for s in range(CMP_STRIDE):
        pieces = []
        for c in range(0, n_chunks, 2):
            a = tile_ref[0, TILES_PER_CHUNK * c + s // 2]
            b = tile_ref[0, TILES_PER_CHUNK * (c + 1) + s // 2]
            if s % 2 == 0:
                pieces.append(jnp.where(low, a, pltpu.roll(b, KV_HEADS, axis=0)))
            else:
                pieces.append(jnp.where(low, pltpu.roll(a, KV_HEADS, axis=0), b))
        cols.append(jnp.concatenate(pieces, axis=0))
    return jnp.concatenate(cols, axis=1)


def _compress_mlp(x, shift, w1_ref, pt_ref, w2_ref, b2_ref):
    n = x.shape[0]
    fs = _dot(x, w1_ref[0])
    first = fs[:, :CMP_HIDDEN]
    second = pltpu.roll(fs[:, CMP_HIDDEN:], n - shift, axis=0)
    pre = first + second + pt_ref[0, 0:1, :]
    hid = (pre * _sigmoid(pre)).astype(BF)
    out = _dot(hid, w2_ref[0]) + b2_ref[0]
    row = lax.broadcasted_iota(jnp.int32, (n, 1), 0)
    return jnp.where(row < n - shift, out, 0.0)


def _compress_rows_kernel(k_ref, v_ref, w1_ref, pt_ref, w2_ref, b2_ref, ok_ref, ov_ref):
    for i, (x_ref, o_ref) in enumerate(((k_ref, ok_ref), (v_ref, ov_ref))):
        w = (w1_ref.at[i:i + 1], pt_ref.at[i:i + 1], w2_ref.at[i:i + 1], b2_ref.at[i:i + 1])
        x = x_ref[0]
        for g in range(KV_HEADS):
            out = _compress_mlp(_chunk_matrix(x, g), 1, *w)
            o_ref[0, :, g * HEAD_DIM:(g + 1) * HEAD_DIM] = out.astype(o_ref.dtype)


def _compress_pages_kernel(*refs, n_pages):
    refs = refs[1:]
    w1_ref, pt_ref, w2_ref, b2_ref, ok_ref, ov_ref = refs[2 * n_pages:]
    for i, o_ref in enumerate((ok_ref, ov_ref)):
        w = (w1_ref.at[i:i + 1], pt_ref.at[i:i + 1], w2_ref.at[i:i + 1], b2_ref.at[i:i + 1])
        pages = refs[i * n_pages:(i + 1) * n_pages]
        x = jnp.concatenate([_page_chunk_rows(p).astype(BF) for p in pages], axis=0)
        o_ref[0] = _compress_mlp(x, KV_HEADS, *w).astype(o_ref.dtype)


def _compress_weight_specs():
    fixed3 = lambda *a: (0, 0, 0)
    return [
        pl.BlockSpec((2, CMP_STRIDE * HEAD_DIM, 2 * CMP_HIDDEN), fixed3),
        pl.BlockSpec((2, 8, CMP_HIDDEN), fixed3),
        pl.BlockSpec((2, CMP_HIDDEN, HEAD_DIM), fixed3),
        pl.BlockSpec((2, 1, HEAD_DIM), fixed3),
    ]


def compress_rows(k_rows, v_rows, cw):
    b, n, width = k_rows.shape
    spec = pl.BlockSpec((1, n, width), lambda i: (i, 0, 0))
    out = jax.ShapeDtypeStruct((b, n, KV_COLS), BF)
    ospec = pl.BlockSpec((1, n, KV_COLS), lambda i: (i, 0, 0))
    return pl.pallas_call(
        _compress_rows_kernel,
        out_shape=(out, out),
        grid=(b,),
        in_specs=[spec, spec] + _compress_weight_specs(),
        out_specs=(ospec, ospec),
        compiler_params=_params("parallel"),
        name="compress_rows",
    )(k_rows, v_rows, *cw)


def compress_pages(page_table, n_pages, cache_k, cache_v, cw):
    n_pool, tiles, rows, hd = cache_k.shape
    b = page_table.shape[0] // n_pages
    n = n_pages * tiles // TILES_PER_CHUNK * KV_HEADS

    def page_spec(p):
        return pl.BlockSpec((1, tiles, rows, hd), lambda i, pt: (pt[i * n_pages + p], 0, 0, 0))

    pages = [page_spec(p) for p in range(n_pages)]
    out = jax.ShapeDtypeStruct((b, n, HEAD_DIM), BF)
    ospec = pl.BlockSpec((1, n, HEAD_DIM), lambda i, pt: (i, 0, 0))
    grid_spec = pltpu.PrefetchScalarGridSpec(
        num_scalar_prefetch=1,
        grid=(b,),
        in_specs=pages + pages + _compress_weight_specs(),
        out_specs=(ospec, ospec),
    )
    return pl.pallas_call(
        functools.partial(_compress_pages_kernel, n_pages=n_pages),
        out_shape=(out, out),
        grid_spec=grid_spec,
        compiler_params=_params("parallel"),
        name="compress_pages",
    )(page_table, *([cache_k] * n_pages), *([cache_v] * n_pages), *cw)


def _softmax_cols(s, mask):
    s = jnp.where(mask, s, NEG_INF)
    m = jnp.max(s, axis=0, keepdims=True)
    m = jnp.where(m == NEG_INF, 0.0, m)
    p = jnp.where(mask, jnp.exp(s - m), 0.0)
    return p / jnp.maximum(jnp.sum(p, axis=0, keepdims=True), 1e-30)


def _top_rank_lanes(score, valid, n):
    idx = lax.broadcasted_iota(jnp.int32, score.shape, 1)
    cnt = jnp.zeros(score.shape, F32)
    for j in range(n):
        other = score[:, j:j + 1]
        tie = jnp.where(idx > j, 1.0, 0.0)
        cnt = cnt + jnp.where(other > score, 1.0, jnp.where(other == score, tie, 0.0))
    return jnp.where(cnt < TOP_N, jnp.where(valid, 1.0, 0.0), 0.0)


def _top_rank_sublanes(score, valid):
    n = score.shape[0]
    groups = [score[v:v + 8, :] for v in range(0, n, 8)]
    cnts = [jnp.zeros(g.shape, F32) for g in groups]
    for j in range(n):
        other = score[j:j + 1, :]
        for v, g in enumerate(groups):
            if 8 * v > j:
                beats = other >= g
            elif 8 * v + 7 <= j:
                beats = other > g
            else:
                sub = 8 * v + lax.broadcasted_iota(jnp.int32, g.shape, 0)
                beats = (other > g) | ((other == g) & (sub > j))
            cnts[v] = cnts[v] + jnp.where(beats, 1.0, 0.0)
    cnt = jnp.concatenate(cnts, axis=0)
    return jnp.where(cnt < TOP_N, jnp.where(valid, 1.0, 0.0), 0.0)


def _flash_update(s, vt, carry):
    m, l, acc = carry
    m_new = jnp.maximum(m, jnp.max(s, axis=0, keepdims=True))
    m_safe = jnp.where(m_new == NEG_INF, 0.0, m_new)
    alpha = jnp.exp(m - m_safe)
    p = jnp.exp(s - m_safe)
    l = alpha * l + jnp.sum(p, axis=0, keepdims=True)
    acc = alpha * acc + _dot(vt, p.astype(BF))
    return m_new, l, acc


def _prompt_attn_kernel(q_ref, gt_ref, kc_ref, vct_ref, ks_ref, vst_ref, kw_ref, vwt_ref,
                        o_ref, sel_ref, *, n_cmp):
    qt = pl.program_id(2)
    n_rows = GROUP * Q_TILE
    q = q_ref[0]
    q_rows = jnp.concatenate([q[:, r * HEAD_DIM:(r + 1) * HEAD_DIM] for r in range(GROUP)], axis=0)
    t0 = qt * Q_TILE
    pos_t = t0 + lax.broadcasted_iota(jnp.int32, (1, Q_TILE), 1)
    pos_rows = jnp.concatenate([pos_t] * GROUP, axis=1)

    n_cpad = kc_ref.shape[1]
    ci = lax.broadcasted_iota(jnp.int32, (n_cpad, 1), 0)
    cmask = (ci * CMP_STRIDE + (CMP_LEN - 1) <= pos_rows) & (ci < n_cmp)
    p_cmp = _softmax_cols(_dot_nt(kc_ref[0], q_rows), cmask)
    o_cmp = _dot(vct_ref[0, 0], p_cmp.astype(BF))

    n_steps = ks_ref.shape[2]
    blk_per_step = KEY_STEP // SEL_BLOCK
    n_sel = n_steps * blk_per_step
    p_sum = p_cmp[:, 0:Q_TILE]
    for r in range(1, GROUP):
        p_sum = p_sum + p_cmp[:, r * Q_TILE:(r + 1) * Q_TILE]
    bj = lax.broadcasted_iota(jnp.int32, (n_sel, n_cpad), 0) * SEL_BLOCK
    bi = lax.broadcasted_iota(jnp.int32, (n_sel, n_cpad), 1) * CMP_STRIDE
    overlap_t = jnp.where((bi < bj + SEL_BLOCK) & (bi + CMP_LEN > bj), 1.0, 0.0).astype(BF)
    hi, mid, lo = _split3(p_sum)
    imp = _dot(overlap_t, hi) + _dot(overlap_t, mid) + _dot(overlap_t, lo)
    blk = lax.broadcasted_iota(jnp.int32, (n_sel, 1), 0)
    cur = pos_t >> SEL_SHIFT
    forced = (blk == 0) | (blk == cur) | (blk == cur - 1)
    causal = blk * SEL_BLOCK <= pos_t
    score = jnp.where(causal, imp + FORCE_BONUS * jnp.where(forced, 1.0, 0.0), NEG_INF)
    sel = _top_rank_sublanes(score, causal)
    for st in range(n_steps):
        sel_ref[st] = sel[st * blk_per_step:(st + 1) * blk_per_step, :]

    init = (jnp.full((1, n_rows), NEG_INF, F32), jnp.zeros((1, n_rows), F32),
            jnp.zeros((HEAD_DIM, n_rows), F32))

    key_in_blk = lax.broadcasted_iota(jnp.int32, (SEL_BLOCK, 1), 0)

    def sel_scores(st):
        picked = sel_ref[st]
        pieces = []
        for h in range(blk_per_step):
            kpos = st * KEY_STEP + h * SEL_BLOCK + key_in_blk
            ok = (kpos <= pos_t) & (picked[h:h + 1, :] > 0.5)
            pieces.append(jnp.where(ok, 0.0, NEG_INF))
        b = jnp.concatenate(pieces, axis=0)
        return _dot_nt(ks_ref[0, 0, st], q_rows) + jnp.concatenate([b] * GROUP, axis=1)

    def sel_step(st, carry):
        return _flash_update(sel_scores(st), vst_ref[0, 0, st], carry)

    n_needed = (t0 + Q_TILE + KEY_STEP - 1) // KEY_STEP
    _, l_sel, acc_sel = lax.fori_loop(0, n_needed, sel_step, init)
    o_sel = acc_sel / jnp.maximum(l_sel, 1e-30)

    n_wt = (WINDOW + Q_TILE) // KEY_TILE
    first = qt * (Q_TILE // KEY_TILE) - WINDOW // KEY_TILE
    tiles = [jnp.maximum(first + i, 0) for i in range(n_wt)]
    k_win = jnp.concatenate([kw_ref[0, 0, kt] for kt in tiles], axis=0)
    vt_win = jnp.concatenate([vwt_ref[0, 0, kt] for kt in tiles], axis=1)
    kpos = first * KEY_TILE + lax.broadcasted_iota(jnp.int32, (n_wt * KEY_TILE, 1), 0)
    ok = (kpos >= 0) & (kpos <= pos_rows) & (pos_rows - kpos < WINDOW)
    s_win = _dot_nt(k_win, q_rows) + jnp.where(ok, 0.0, NEG_INF)
    _, l_win, acc_win = _flash_update(s_win, vt_win, init)
    o_win = acc_win / jnp.maximum(l_win, 1e-30)

    gt = gt_ref[0]
    for r in range(GROUP):
        cols = slice(r * Q_TILE, (r + 1) * Q_TILE)
        base = r * N_BRANCH
        mix = (gt[base:base + 1, :] * o_cmp[:, cols] + gt[base + 1:base + 2, :] * o_sel[:, cols]
               + gt[base + 2:base + 3, :] * o_win[:, cols])
        o_ref[0, :, r * HEAD_DIM:(r + 1) * HEAD_DIM] = mix.T.astype(o_ref.dtype)


def prompt_attention(q, gates_t, kc, vct, ks, vst, kw, vwt, n_cmp):
    b, t, _ = q.shape
    n_q = t // Q_TILE
    n_steps = ks.shape[2]
    n_kt = kw.shape[2]
    n_cpad = kc.shape[1]
    whole = lambda i, g, j: (i, g, 0, 0, 0)
    width = GROUP * HEAD_DIM
    return pl.pallas_call(
        functools.partial(_prompt_attn_kernel, n_cmp=n_cmp),
        out_shape=jax.ShapeDtypeStruct((b, t, N_HEADS * HEAD_DIM), BF),
        grid=(b, KV_HEADS, n_q),
        in_specs=[
            pl.BlockSpec((1, Q_TILE, width), lambda i, g, j: (i, j, g)),
            pl.BlockSpec((1, GROUP * N_BRANCH, Q_TILE), lambda i, g, j: (g, 0, i * n_q + j)),
            pl.BlockSpec((1, n_cpad, HEAD_DIM), lambda i, g, j: (i, 0, g)),
            pl.BlockSpec((1, 1, HEAD_DIM, n_cpad), lambda i, g, j: (i, g, 0, 0)),
            pl.BlockSpec((1, 1, n_steps, KEY_STEP, HEAD_DIM), whole),
            pl.BlockSpec((1, 1, n_steps, HEAD_DIM, KEY_STEP), whole),
            pl.BlockSpec((1, 1, n_kt, KEY_TILE, HEAD_DIM), whole),
            pl.BlockSpec((1, 1, n_kt, HEAD_DIM, KEY_TILE), whole),
        ],
        out_specs=pl.BlockSpec((1, Q_TILE, width), lambda i, g, j: (i, j, g)),
        scratch_shapes=[pltpu.VMEM((n_steps, KEY_STEP // SEL_BLOCK, Q_TILE), F32)],
        compiler_params=_params("parallel", "parallel", "arbitrary"),
        name="prompt_attention",
    )(q, gates_t, kc, vct, ks, vst, kw, vwt)


def _softmax_two(s_a, bias_a, s_b, bias_b):
    s_a = s_a + bias_a
    s_b = s_b + bias_b
    m = jnp.maximum(jnp.max(s_a, axis=1, keepdims=True), jnp.max(s_b, axis=1, keepdims=True))
    m = jnp.where(m == NEG_INF, 0.0, m)
    p_a = jnp.exp(s_a - m)
    p_b = jnp.exp(s_b - m)
    total = jnp.sum(p_a, axis=1, keepdims=True) + jnp.sum(p_b, axis=1, keepdims=True)
    return p_a, p_b, 1.0 / jnp.maximum(total, 1e-30)


def _sample_attn_kernel(*refs, n_pages, t_new, past_len, n_cmp):
    refs = refs[1:]
    q_ref, gate_ref, kc_ref, vc_ref = refs[:4]
    ks_pages = refs[4:4 + n_pages]
    vs_pages = refs[4 + n_pages:4 + 2 * n_pages]
    (ksn_ref, vsn_ref, kwp_ref, vwp_ref, kwn_ref, vwn_ref, o_ref,
     expand_ref, past_bias_ref, win_bias_ref) = refs[4 + 2 * n_pages:]

    n_rows = q_ref.shape[1]
    rows_per_head = n_rows // KV_HEADS
    q = q_ref[0]
    row = lax.broadcasted_iota(jnp.int32, (n_rows, 1), 0)
    row_head = _div(row, rows_per_head)
    pos = past_len + _mod(row, t_new)
    n_past = n_pages * ks_pages[0].shape[1]
    n_win = kwp_ref.shape[1]

    def col_key(n_cols):
        return _div(lax.broadcasted_iota(jnp.int32, (1, n_cols), 1), KV_HEADS)

    def head_bias(n_cols):
        col = lax.broadcasted_iota(jnp.int32, (1, n_cols), 1)
        return jnp.where(_mod(col, KV_HEADS) == row_head, 0.0, NEG_INF)

    @pl.when(pl.program_id(0) == 0)
    def _():
        ej = lax.broadcasted_iota(jnp.int32, (128, n_past), 0)
        ek = _div(lax.broadcasted_iota(jnp.int32, (128, n_past), 1), KV_HEADS)
        expand_ref[...] = jnp.where((ek >> SEL_SHIFT) == ej, 1.0, 0.0).astype(BF)
        past_bias_ref[...] = jnp.where(col_key(n_past) <= pos, head_bias(n_past), NEG_INF)
        wpos = past_len - n_win // KV_HEADS + col_key(n_win)
        win_bias_ref[...] = jnp.where((wpos <= pos) & (pos - wpos < WINDOW), head_bias(n_win), NEG_INF)

    new_pos = past_len + col_key(128)
    new_bias = jnp.where((col_key(128) < t_new) & (new_pos <= pos), head_bias(128), NEG_INF)

    def pad_new(ref):
        x = ref[0]
        return jnp.concatenate([x, jnp.zeros((128 - x.shape[0], x.shape[1]), x.dtype)], axis=0).astype(BF)

    n_c = kc_ref.shape[1]
    chunk = col_key(n_c)
    c_ok = (chunk * CMP_STRIDE + (CMP_LEN - 1) <= pos) & (chunk < n_cmp)
    s_c = _dot_nt(q, kc_ref[0]) + jnp.where(c_ok, head_bias(n_c), NEG_INF)
    m_c = jnp.max(s_c, axis=1, keepdims=True)
    m_c = jnp.where(m_c == NEG_INF, 0.0, m_c)
    p_c = jnp.exp(s_c - m_c)
    p_c = p_c / jnp.maximum(jnp.sum(p_c, axis=1, keepdims=True), 1e-30)
    o_cmp = _dot(p_c.astype(BF), vc_ref[0])

    ra = lax.broadcasted_iota(jnp.int32, (n_rows, n_rows), 0)
    rb = lax.broadcasted_iota(jnp.int32, (n_rows, n_rows), 1)
    same_gt = ((_div(ra, rows_per_head) == _div(rb, rows_per_head))
               & (_mod(ra, t_new) == _mod(rb, t_new)))
    group_sum = jnp.where(same_gt, 1.0, 0.0).astype(BF)
    hi, mid, lo = _split3(p_c)
    p_sum = _dot(group_sum, hi) + _dot(group_sum, mid) + _dot(group_sum, lo)
    n_keys = past_len + t_new
    n_sel = -(-n_keys // SEL_BLOCK)
    bi = _div(lax.broadcasted_iota(jnp.int32, (n_c, 128), 0), KV_HEADS) * CMP_STRIDE
    bj = lax.broadcasted_iota(jnp.int32, (n_c, 128), 1) * SEL_BLOCK
    overlap = jnp.where((bi < bj + SEL_BLOCK) & (bi + CMP_LEN > bj), 1.0, 0.0).astype(BF)
    imp = _dot01(p_sum, overlap)
    blk = lax.broadcasted_iota(jnp.int32, (1, 128), 1)
    cur = pos >> SEL_SHIFT
    forced = (blk == 0) | (blk == cur) | (blk == cur - 1)
    causal = (blk * SEL_BLOCK <= pos) & (blk < n_sel)
    score = jnp.where(causal, imp + FORCE_BONUS * jnp.where(forced, 1.0, 0.0), NEG_INF)
    sel = _top_rank_lanes(score, causal, n_sel)

    k_past = jnp.concatenate([p[0].astype(BF) for p in ks_pages], axis=0)
    v_past = jnp.concatenate([p[0].astype(BF) for p in vs_pages], axis=0)
    sel16 = sel.astype(BF)
    bias_past = jnp.where(_dot(sel16, expand_ref[...]) > 0.5, past_bias_ref[...], NEG_INF)
    nj = lax.broadcasted_iota(jnp.int32, (128, 128), 0)
    nk = past_len + _div(lax.broadcasted_iota(jnp.int32, (128, 128), 1), KV_HEADS)
    expand_new = jnp.where((nk >> SEL_SHIFT) == nj, 1.0, 0.0).astype(BF)
    bias_new = jnp.where(_dot(sel16, expand_new) > 0.5, new_bias, NEG_INF)
    p_a, p_b, inv = _softmax_two(_dot_nt(q, k_past), bias_past, _dot_nt(q, pad_new(ksn_ref)), bias_new)
    o_sel = (_dot(p_a.astype(BF), v_past) + _dot(p_b.astype(BF), pad_new(vsn_ref))) * inv

    bias_wn = jnp.where(pos - new_pos < WINDOW, new_bias, NEG_INF)
    p_a, p_b, inv = _softmax_two(_dot_nt(q, kwp_ref[0].astype(BF)), win_bias_ref[...],
                                 _dot_nt(q, pad_new(kwn_ref)), bias_wn)
    o_win = (_dot(p_a.astype(BF), vwp_ref[0].astype(BF)) + _dot(p_b.astype(BF), pad_new(vwn_ref))) * inv

    gate = gate_ref[0]
    o = gate[:, 0:1] * o_cmp + gate[:, 1:2] * o_sel + gate[:, 2:3] * o_win
    o_ref[0] = o.astype(o_ref.dtype)


def sample_attention(page_table, q, gates, kc, vc, cache_ks, cache_vs, ks_new, vs_new,
                     kw_past, vw_past, kw_new, vw_new, *, t_new, n_cmp):
    b, n_rows, _ = q.shape
    n_pages = page_table.shape[0] // b
    page_rows = cache_ks.shape[1]
    n_c = kc.shape[1]
    n_win = kw_past.shape[1]
    n_new = ks_new.shape[1]
    n_past = n_pages * page_rows

    def page_spec(p):
        return pl.BlockSpec((1, page_rows, HEAD_DIM), lambda i, pt: (pt[i * n_pages + p], 0, 0))

    per_seq = lambda rows, cols=HEAD_DIM: pl.BlockSpec((1, rows, cols), lambda i, pt: (i, 0, 0))
    pages = [page_spec(p) for p in range(n_pages)]
    grid_spec = pltpu.PrefetchScalarGridSpec(
        num_scalar_prefetch=1,
        grid=(b,),
        in_specs=[per_seq(n_rows), per_seq(n_rows, N_BRANCH), per_seq(n_c), per_seq(n_c)]
        + pages + pages
        + [per_seq(n_new), per_seq(n_new), per_seq(n_win), per_seq(n_win), per_seq(n_new), per_seq(n_new)],
        out_specs=per_seq(n_rows),
        scratch_shapes=[pltpu.VMEM((128, n_past), BF), pltpu.VMEM((n_rows, n_past), F32),
                        pltpu.VMEM((n_rows, n_win), F32)],
    )
    kern = functools.partial(_sample_attn_kernel, n_pages=n_pages, t_new=t_new,
                             past_len=n_past // KV_HEADS, n_cmp=n_cmp)
    return pl.pallas_call(
        kern,
        out_shape=jax.ShapeDtypeStruct((b, n_rows, HEAD_DIM), BF),
        grid_spec=grid_spec,
        compiler_params=_params("arbitrary"),
        name="sample_attention",
    )(page_table, q, gates, kc, vc, *([cache_ks] * n_pages), *([cache_vs] * n_pages),
      ks_new, vs_new, kw_past, vw_past, kw_new, vw_new)


def _window_update_kernel(k_ref, v_ref, kn_ref, vn_ref, ok_ref, ov_ref):
    n = k_ref.shape[1]
    n_new = kn_ref.shape[1]
    for s_ref, n_ref, o_ref in ((k_ref, kn_ref, ok_ref), (v_ref, vn_ref, ov_ref)):
        o_ref[:, 0:n - n_new, :] = s_ref[:, n_new:n, :]
        o_ref[:, n - n_new:n, :] = n_ref[...]


def window_update(k_state, v_state, k_new, v_new):
    b, n, d = k_state.shape
    n_new = k_new.shape[1]
    bb = 2
    state = pl.BlockSpec((bb, n, d), lambda i: (i, 0, 0))
    new = pl.BlockSpec((bb, n_new, d), lambda i: (i, 0, 0))
    out = jax.ShapeDtypeStruct((b, n, d), F32)
    return pl.pallas_call(
        _window_update_kernel,
        out_shape=(out, out),
        grid=(b // bb,),
        in_specs=[state, state, new, new],
        out_specs=(state, state),
        compiler_params=_params("parallel"),
        name="window_update",
    )(k_state, v_state, k_new, v_new)


def _rope_tables(pos):
    half = HEAD_DIM // 2
    inv = ROPE_THETA ** (-jnp.arange(half, dtype=F32) / half)
    ang = pos.astype(F32)[:, None] * inv[None, :]
    cos = jnp.cos(ang)
    sin = jnp.sin(ang)
    return jnp.concatenate([cos, cos], axis=1), jnp.concatenate([-sin, sin], axis=1)


def kernel(x_prompt, x_sample, cache_k_cmp, cache_v_cmp, cache_k_sel, cache_v_sel, state_k_win, state_v_win, state_conv, page_table, norm_g, conv_w1, conv_b1, conv_wdw, conv_bdw, conv_ln_g, conv_ln_b, conv_w2, conv_b2, kv_norm_g, w_kv, cmp_w1, cmp_pe, cmp_w2, cmp_b2, nsa_w_qg, nsa_w_o, ffn_w_in, ffn_w_out):
    bp, tp, d = x_prompt.shape
    bs, ts, _ = x_sample.shape
    n_p = bp * tp
    n_s = bs * ts
    n_pool, page, _, _ = cache_k_sel.shape
    n_pages = page_table.shape[1]
    past_len = n_pages * page
    bf = lambda w: w.astype(BF)

    def split(a):
        return a[:n_p], a[n_p:]

    x_in = (x_prompt.reshape(n_p, d), x_sample.reshape(n_s, d))
    pos = jnp.concatenate([jnp.tile(jnp.arange(tp, dtype=jnp.int32), bp),
                           jnp.tile(past_len + jnp.arange(ts, dtype=jnp.int32), bs)])
    cos, sin = _rope_tables(pos)

    g = norm_g[0]
    u = glu_in(x_in, g[0], bf(conv_w1[0]), conv_b1[0])
    conv_tile = 128
    assert tp % conv_tile == 0 and (n_p + n_s) % conv_tile == 0
    u_s = u[n_p:].reshape(bs, ts, d)
    conv_args = (conv_wdw[0], conv_bdw[0], conv_ln_g[0], conv_ln_b[0])
    v_p = conv_ln(jnp.zeros((bp, CONV_HALO, d), F32), u.reshape(-1, conv_tile, d), *conv_args,
                  bb=1, tt=conv_tile, n_t=tp // conv_tile)
    v_s = conv_ln(state_conv[0], u_s, *conv_args, bb=16, tt=ts)
    x = mm_norm_res((v_p.reshape(n_p, d), v_s.reshape(n_s, d)), bf(conv_w2[0]), conv_b2[0], g[1], x_in)
    ffn_in16, ffn_out16 = bf(ffn_w_in), bf(ffn_w_out)
    x = ffn(x, g[2], ffn_in16, ffn_out16, g[3], 0)
    conv_p = u[:n_p].reshape(bp, tp, d)[:, tp - CONV_HALO:][None]
    conv_s = jnp.concatenate([state_conv[0][:, ts:], u_s], axis=1)[None]

    kv32, kv16 = kv_proj(x, kv_norm_g, bf(w_kv), cos, sin)
    rows_p = [kv32[s, :n_p].reshape(bp, tp, KV_HEADS, HEAD_DIM) for s in range(N_KV_SLOTS)]
    rows_s = [kv32[s, n_p:].reshape(bs, ts, KV_HEADS, HEAD_DIM) for s in range(N_KV_SLOTS)]
    n_wp = min(WINDOW, tp)
    kw_p, vw_p = rows_p[4][:, tp - n_wp:], rows_p[5][:, tp - n_wp:]

    def new_rows(slot):
        return kv32[slot, n_p:].reshape(bs, ts * KV_HEADS, HEAD_DIM)

    head_rows = lambda a: a.reshape(a.shape[0], -1, HEAD_DIM)
    kw_s, vw_s = window_update(head_rows(state_k_win), head_rows(state_v_win), new_rows(4), new_rows(5))
    kw_s, vw_s = kw_s.reshape(state_k_win.shape), vw_s.reshape(state_v_win.shape)

    w1 = cmp_w1.reshape(2, 2, CMP_STRIDE * HEAD_DIM, CMP_HIDDEN)
    w1cat = bf(jnp.concatenate([w1[:, 0], w1[:, 1]], axis=-1))
    cw = (w1cat, pos_term(cmp_pe, cmp_w1), bf(cmp_w2), cmp_b2.reshape(2, 1, HEAD_DIM))
    chunk_cols = CMP_STRIDE * KV_COLS
    kc_p, vc_p = compress_rows(kv16[0, :n_p].reshape(bp, tp // CMP_STRIDE, chunk_cols),
                               kv16[1, :n_p].reshape(bp, tp // CMP_STRIDE, chunk_cols), cw)
    pt_flat = page_table.reshape(-1)
    assert (past_len + ts) // CMP_STRIDE == past_len // CMP_STRIDE
    page_rows = page * KV_HEADS
    as_tiles = lambda c: c.reshape(n_pool, page_rows // TILE_ROWS, TILE_ROWS, HEAD_DIM)
    kc_s, vc_s = compress_pages(pt_flat, n_pages, as_tiles(cache_k_cmp), as_tiles(cache_v_cmp), cw)

    g = norm_g[1]
    w_qg = nsa_w_qg[0]
    n_q_cols = N_HEADS * HEAD_DIM
    n_gate = N_HEADS * N_BRANCH
    wg = w_qg[:, n_q_cols:]
    wg_pad = bf(jnp.pad(wg, ((0, 0), (0, 128 - n_gate))))
    q, gates, gates_t = q_proj(x, g[0], bf(w_qg[:, :n_q_cols]), wg_pad, bf(wg.T), cos, sin)

    assert tp % KEY_STEP == 0 and tp % Q_TILE == 0

    def key_tiles(a, tk):
        return a.reshape(bp, tp // tk, tk, KV_HEADS, HEAD_DIM).transpose(0, 3, 1, 2, 4)

    def value_tiles_t(a, tk):
        return a.reshape(bp, tp // tk, tk, KV_HEADS, HEAD_DIM).transpose(0, 3, 1, 4, 2)

    n_cmp_p = tp // CMP_STRIDE - 1
    vct_p = vc_p.reshape(bp, tp // CMP_STRIDE, KV_HEADS, HEAD_DIM).transpose(0, 2, 3, 1)
    gates_t_p = gates_t[:, :n_p].reshape(KV_HEADS, GROUP * N_BRANCH, n_p)
    o_p = prompt_attention(q[:n_p].reshape(bp, tp, n_q_cols), gates_t_p, kc_p, vct_p,
                           key_tiles(kv16[2, :n_p], KEY_STEP), value_tiles_t(kv16[3, :n_p], KEY_STEP),
                           key_tiles(kv16[4, :n_p], KEY_TILE), value_tiles_t(kv16[5, :n_p], KEY_TILE), n_cmp_p)

    def seq_rows(a, width):
        return a.reshape(bs, ts, KV_HEADS, GROUP, width).transpose(0, 2, 3, 1, 4).reshape(bs, -1, width)

    q_s = seq_rows(q[n_p:], HEAD_DIM)
    gates_s = seq_rows(gates[n_p:, :n_gate], N_BRANCH)

    o_s = sample_attention(
        pt_flat, q_s, gates_s, kc_s, vc_s,
        head_rows(cache_k_sel), head_rows(cache_v_sel), new_rows(2), new_rows(3),
        head_rows(state_k_win), head_rows(state_v_win), new_rows(4), new_rows(5), t_new=ts, n_cmp=past_len // CMP_STRIDE - 1)
    o_s = o_s.reshape(bs, KV_HEADS, GROUP, ts, HEAD_DIM).transpose(0, 3, 1, 2, 4).reshape(n_s, n_q_cols)
    x = mm_norm_res((o_p.reshape(n_p, n_q_cols), o_s), bf(nsa_w_o[0]), None, g[1], x)
    x = ffn(x, g[2], ffn_in16, ffn_out16, g[3], 1)

    y_p, y_s = split(x)
    return (y_p.reshape(bp, tp, d), y_s.reshape(bs, ts, d), conv_p,
            rows_p[0], rows_p[1], rows_p[2], rows_p[3], kw_p, vw_p,
            conv_s, rows_s[0], rows_s[1], rows_s[2], rows_s[3], kw_s, vw_s)
```

```python
import functools

import jax
import jax.numpy as jnp
from jax import lax
from jax.experimental import pallas as pl
from jax.experimental.pallas import tpu as pltpu

D_MODEL = 2048
CONV_WIDTH = 31
CONV_HALO = CONV_WIDTH - 1
N_HEADS = 16
HEAD_DIM = 128
KV_HEADS = 4
GROUP = N_HEADS // KV_HEADS
N_BRANCH = 3
N_KV_SLOTS = 6
KV_COLS = KV_HEADS * HEAD_DIM
CMP_LEN = 32
CMP_STRIDE = 16
CMP_HIDDEN = 256
SEL_BLOCK = 64
TOP_N = 16
WINDOW = 512
ROPE_THETA = 10000.0
FORCE_BONUS = 1.0e4
EPS = 1e-6
SCALE = HEAD_DIM ** -0.5
SEL_SHIFT = SEL_BLOCK.bit_length() - 1
assert 1 << SEL_SHIFT == SEL_BLOCK

TILE_ROWS = 8
BF = jnp.bfloat16
F32 = jnp.float32
NEG_INF = float("-inf")

VMEM_LIMIT_BYTES = 56 * 1024 * 1024
ROW_TILE = 512
KEY_TILE = 128
KEY_STEP = 1024
Q_TILE = 256


def _params(*sem):
    return pltpu.CompilerParams(dimension_semantics=sem, vmem_limit_bytes=VMEM_LIMIT_BYTES)


def _sigmoid(x):
    return 1.0 / (1.0 + jnp.exp(-x))


def _rms(x, g):
    return x * lax.rsqrt(jnp.mean(x * x, axis=-1, keepdims=True) + EPS) * g


def _rope(y, cos, sin_signed):
    return y * cos + pltpu.roll(y, HEAD_DIM // 2, axis=1) * sin_signed


def _split3(x):
    hi = x.astype(BF)
    r1 = x - hi.astype(F32)
    mid = r1.astype(BF)
    lo = (r1 - mid.astype(F32)).astype(BF)
    return hi, mid, lo


def _div(x, n):
    assert n & (n - 1) == 0
    return x >> (n.bit_length() - 1)


def _mod(x, n):
    assert n & (n - 1) == 0
    return x & (n - 1)


def _dot(a, b):
    return jnp.dot(a, b, preferred_element_type=F32)


def _dot_nt(a, b):
    return lax.dot_general(a, b, (((1,), (1,)), ((), ())), preferred_element_type=F32)


def _dot01(x, onehot):
    hi, mid, lo = _split3(x)
    return _dot(hi, onehot) + _dot(mid, onehot) + _dot(lo, onehot)


def _row_pair_specs(n_first, block):
    first = pl.BlockSpec(block, lambda i, *_: (jnp.minimum(i, n_first - 1), 0))
    second = pl.BlockSpec(block, lambda i, *_: (jnp.maximum(i - n_first, 0), 0))
    return [first, second]


def _glu_in_kernel(xa_ref, xb_ref, g_ref, wa_ref, wb_ref, ba_ref, bb_ref, o_ref, xn_ref, *, n_first):
    i = pl.program_id(0)
    j = pl.program_id(1)

    @pl.when((j == 0) & (i < n_first))
    def _():
        xn_ref[...] = _rms(xa_ref[...], g_ref[...]).astype(BF)

    @pl.when((j == 0) & (i >= n_first))
    def _():
        xn_ref[...] = _rms(xb_ref[...], g_ref[...]).astype(BF)

    xn = xn_ref[...]
    a = _dot(xn, wa_ref[...]) + ba_ref[...]
    b = _dot(xn, wb_ref[...]) + bb_ref[...]
    o_ref[...] = a * _sigmoid(b)


def glu_in(x_pair, g, w, b):
    xa, xb = x_pair
    d = xa.shape[1]
    m = xa.shape[0] + xb.shape[0]
    n_first = xa.shape[0] // ROW_TILE
    tn = 512
    nj = d // tn
    return pl.pallas_call(
        functools.partial(_glu_in_kernel, n_first=n_first),
        out_shape=jax.ShapeDtypeStruct((m, d), F32),
        grid=(m // ROW_TILE, nj),
        in_specs=_row_pair_specs(n_first, (ROW_TILE, d)) + [
            pl.BlockSpec((1, d), lambda i, j: (0, 0)),
            pl.BlockSpec((d, tn), lambda i, j: (0, j)),
            pl.BlockSpec((d, tn), lambda i, j: (0, j + nj)),
            pl.BlockSpec((1, tn), lambda i, j: (0, j)),
            pl.BlockSpec((1, tn), lambda i, j: (0, j + nj)),
        ],
        out_specs=pl.BlockSpec((ROW_TILE, tn), lambda i, j: (i, j)),
        scratch_shapes=[pltpu.VMEM((ROW_TILE, d), BF)],
        compiler_params=_params("parallel", "arbitrary"),
        name="glu_in",
    )(xa, xb, g.reshape(1, d), w, w, b.reshape(1, 2 * d), b.reshape(1, 2 * d))


def _mm_norm_res_kernel(*refs, has_bias, x_is_pair, n_first):
    ha_ref, hb_ref, w_ref = refs[:3]
    refs = refs[3:]
    if has_bias:
        b_ref, refs = refs[0], refs[1:]
    g_ref, refs = refs[0], refs[1:]
    x_refs, (o_ref, h_ref) = refs[:-2], refs[-2:]
    is_first = pl.program_id(0) < n_first

    @pl.when(is_first)
    def _():
        h_ref[...] = ha_ref[...].astype(BF)

    @pl.when(jnp.logical_not(is_first))
    def _():
        h_ref[...] = hb_ref[...].astype(BF)

    y = _dot(h_ref[...], w_ref[...])
    if has_bias:
        y = y + b_ref[...]
    r = _rms(y, g_ref[...])
    if x_is_pair:
        @pl.when(is_first)
        def _():
            o_ref[...] = x_refs[0][...] + r

        @pl.when(jnp.logical_not(is_first))
        def _():
            o_ref[...] = x_refs[1][...] + r
    else:
        o_ref[...] = x_refs[0][...] + r


def mm_norm_res(h_pair, w, b, g, x):
    ha, hb = h_pair
    k = ha.shape[1]
    m = ha.shape[0] + hb.shape[0]
    d = w.shape[1]
    tm = 256
    n_first = ha.shape[0] // tm
    row = lambda i: (i, 0)
    fixed = lambda i: (0, 0)
    in_specs = _row_pair_specs(n_first, (tm, k)) + [pl.BlockSpec((k, d), fixed)]
    args = [ha, hb, w]
    if b is not None:
        in_specs.append(pl.BlockSpec((1, d), fixed))
        args.append(b.reshape(1, d))
    in_specs.append(pl.BlockSpec((1, d), fixed))
    args.append(g.reshape(1, d))
    x_is_pair = isinstance(x, tuple)
    if x_is_pair:
        in_specs += _row_pair_specs(n_first, (tm, d))
        args += list(x)
    else:
        in_specs.append(pl.BlockSpec((tm, d), row))
        args.append(x)
    return pl.pallas_call(
        functools.partial(_mm_norm_res_kernel, has_bias=b is not None, x_is_pair=x_is_pair, n_first=n_first),
        out_shape=jax.ShapeDtypeStruct((m, d), F32),
        grid=(m // tm,),
        in_specs=in_specs,
        out_specs=pl.BlockSpec((tm, d), row),
        scratch_shapes=[pltpu.VMEM((tm, k), BF)],
        compiler_params=_params("parallel"),
        name="mm_norm_res",
    )(*args)


def _ffn_kernel(x_ref, gi_ref, wa_ref, wb_ref, wo_ref, go_ref, o_ref, xn_ref, acc_ref):
    j = pl.program_id(1)

    @pl.when(j == 0)
    def _():
        xn_ref[...] = _rms(x_ref[...], gi_ref[...]).astype(BF)
        acc_ref[...] = jnp.zeros_like(acc_ref)

    xn = xn_ref[...]
    a = _dot(xn, wa_ref[0])
    b = _dot(xn, wb_ref[0])
    h = (a * _sigmoid(a) * b).astype(BF)
    acc_ref[...] += _dot(h, wo_ref[0])

    @pl.when(j == pl.num_programs(1) - 1)
    def _():
        o_ref[...] = x_ref[...] + _rms(acc_ref[...], go_ref[...])


def ffn(x, g_in, w_in, w_out, g_out, layer):
    m, d = x.shape
    f = w_out.shape[1]
    tf = 512
    nj = f // tf
    return pl.pallas_call(
        _ffn_kernel,
        out_shape=jax.ShapeDtypeStruct((m, d), F32),
        grid=(m // ROW_TILE, nj),
        in_specs=[
            pl.BlockSpec((ROW_TILE, d), lambda i, j: (i, 0)),
            pl.BlockSpec((1, d), lambda i, j: (0, 0)),
            pl.BlockSpec((1, d, tf), lambda i, j: (layer, 0, j)),
            pl.BlockSpec((1, d, tf), lambda i, j: (layer, 0, j + nj)),
            pl.BlockSpec((1, tf, d), lambda i, j: (layer, j, 0)),
            pl.BlockSpec((1, d), lambda i, j: (0, 0)),
        ],
        out_specs=pl.BlockSpec((ROW_TILE, d), lambda i, j: (i, 0)),
        scratch_shapes=[pltpu.VMEM((ROW_TILE, d), BF), pltpu.VMEM((ROW_TILE, d), F32)],
        compiler_params=_params("parallel", "arbitrary"),
        name="ffn",
    )(x, g_in.reshape(1, d), w_in, w_in, w_out, g_out.reshape(1, d))


def _kv_proj_kernel(x_ref, g_ref, w_ref, cos_ref, sin_ref, o32_ref, o16_ref, xn_ref):
    s = pl.program_id(1)

    @pl.when(s == 0)
    def _():
        xn_ref[...] = _rms(x_ref[...], g_ref[...]).astype(BF)

    y = _dot(xn_ref[...], w_ref[...])

    @pl.when(s % 2 == 0)
    def _():
        cos = cos_ref[...]
        sin = sin_ref[...]
        for h in range(KV_HEADS):
            cols = slice(h * HEAD_DIM, (h + 1) * HEAD_DIM)
            r = _rope(y[:, cols], cos, sin)
            o32_ref[0, :, h, :] = r
            o16_ref[0, :, cols] = r.astype(BF)

    @pl.when(s % 2 == 1)
    def _():
        for h in range(KV_HEADS):
            o32_ref[0, :, h, :] = y[:, h * HEAD_DIM:(h + 1) * HEAD_DIM]
        o16_ref[0] = y.astype(BF)


def kv_proj(x, g, w, cos, sin):
    m, d = x.shape
    out = jax.ShapeDtypeStruct((N_KV_SLOTS, m, KV_HEADS, HEAD_DIM), F32)
    out16 = jax.ShapeDtypeStruct((N_KV_SLOTS, m, KV_COLS), BF)
    return pl.pallas_call(
        _kv_proj_kernel,
        out_shape=(out, out16),
        grid=(m // ROW_TILE, N_KV_SLOTS),
        in_specs=[
            pl.BlockSpec((ROW_TILE, d), lambda i, s: (i, 0)),
            pl.BlockSpec((1, d), lambda i, s: (0, 0)),
            pl.BlockSpec((d, KV_COLS), lambda i, s: (0, s)),
            pl.BlockSpec((ROW_TILE, HEAD_DIM), lambda i, s: (i, 0)),
            pl.BlockSpec((ROW_TILE, HEAD_DIM), lambda i, s: (i, 0)),
        ],
        out_specs=(pl.BlockSpec((1, ROW_TILE, KV_HEADS, HEAD_DIM), lambda i, s: (s, i, 0, 0)),
                   pl.BlockSpec((1, ROW_TILE, KV_COLS), lambda i, s: (s, i, 0))),
        scratch_shapes=[pltpu.VMEM((ROW_TILE, d), BF)],
        compiler_params=_params("parallel", "arbitrary"),
        name="kv_proj",
    )(x, g.reshape(1, d), w, cos, sin)


def _q_proj_kernel(x_ref, g_ref, wq_ref, wg_ref, wgt_ref, cos_ref, sin_ref,
                   q_ref, gate_ref, gatet_ref, xn_ref):
    @pl.when(pl.program_id(1) == 0)
    def _():
        xn = _rms(x_ref[...], g_ref[...]).astype(BF)
        xn_ref[...] = xn
        gate_ref[...] = _sigmoid(_dot(xn, wg_ref[...]))
        gatet_ref[...] = _sigmoid(_dot_nt(wgt_ref[...], xn))

    y = _dot(xn_ref[...], wq_ref[...])
    cos = cos_ref[...]
    sin = sin_ref[...]
    for h in range(GROUP):
        cols = slice(h * HEAD_DIM, (h + 1) * HEAD_DIM)
        q_ref[:, cols] = (_rope(y[:, cols], cos, sin) * SCALE).astype(BF)


def q_proj(x, g, wq, wg_pad, wg_t, cos, sin):
    m, d = x.shape
    n_gate = wg_t.shape[0]
    tn = GROUP * HEAD_DIM
    return pl.pallas_call(
        _q_proj_kernel,
        out_shape=(jax.ShapeDtypeStruct((m, N_HEADS * HEAD_DIM), BF),
                   jax.ShapeDtypeStruct((m, 128), F32),
                   jax.ShapeDtypeStruct((n_gate, m), F32)),
        grid=(m // ROW_TILE, KV_HEADS),
        in_specs=[
            pl.BlockSpec((ROW_TILE, d), lambda i, j: (i, 0)),
            pl.BlockSpec((1, d), lambda i, j: (0, 0)),
            pl.BlockSpec((d, tn), lambda i, j: (0, j)),
            pl.BlockSpec((d, 128), lambda i, j: (0, 0)),
            pl.BlockSpec((n_gate, d), lambda i, j: (0, 0)),
            pl.BlockSpec((ROW_TILE, HEAD_DIM), lambda i, j: (i, 0)),
            pl.BlockSpec((ROW_TILE, HEAD_DIM), lambda i, j: (i, 0)),
        ],
        out_specs=(pl.BlockSpec((ROW_TILE, tn), lambda i, j: (i, j)),
                   pl.BlockSpec((ROW_TILE, 128), lambda i, j: (i, 0)),
                   pl.BlockSpec((n_gate, ROW_TILE), lambda i, j: (0, i))),
        scratch_shapes=[pltpu.VMEM((ROW_TILE, d), BF)],
        compiler_params=_params("parallel", "arbitrary"),
        name="q_proj",
    )(x, g.reshape(1, d), wq, wg_pad, wg_t, cos, sin)


CONV_PAD = 32


def _conv_kernel(buf_ref, u_ref, w_ref, bdw_ref, lg_ref, lb_ref, o_ref, nb_ref, full_ref, y_ref,
                 *, tt, n_t, col_chunk):
    lo = CONV_PAD - CONV_HALO
    if n_t > 1:
        t = pl.program_id(1)

        @pl.when(t == 0)
        def _():
            full_ref[:, lo:CONV_PAD, :] = buf_ref[...]

        @pl.when(t > 0)
        def _():
            full_ref[:, lo:CONV_PAD, :] = full_ref[:, tt + lo:tt + CONV_PAD, :]
    else:
        full_ref[:, lo:CONV_PAD, :] = buf_ref[...]
    full_ref[:, CONV_PAD:CONV_PAD + tt, :] = u_ref[...]

    if n_t > 1:
        @pl.when(t == n_t - 1)
        def _():
            nb_ref[...] = full_ref[:, tt + lo:tt + CONV_PAD, :]
    else:
        nb_ref[...] = full_ref[:, tt + lo:tt + CONV_PAD, :]

    d = u_ref.shape[-1]
    if tt % TILE_ROWS == 0:
        full_ref[:, CONV_PAD + tt:, :] = jnp.zeros((full_ref.shape[0], TILE_ROWS, d), F32)
        for c0 in range(0, d, col_chunk):
            cols = slice(c0, c0 + col_chunk)
            acc = None
            for rem in range(TILE_ROWS):
                z = None
                for k in range(CONV_WIDTH):
                    if (lo + k) % TILE_ROWS != rem:
                        continue
                    base = lo + k - rem
                    term = w_ref[k:k + 1, cols] * full_ref[:, base:base + tt + TILE_ROWS, cols]
                    z = term if z is None else z + term
                z = z[:, rem:rem + tt, :]
                acc = z if acc is None else acc + z
            y_ref[:, :, cols] = acc + bdw_ref[:, cols]
    else:
        for c0 in range(0, d, col_chunk):
            cols = slice(c0, c0 + col_chunk)
            acc = None
            for k in range(CONV_WIDTH):
                term = w_ref[k:k + 1, cols] * full_ref[:, lo + k:lo + k + tt, cols]
                acc = term if acc is None else acc + term
            y_ref[:, :, cols] = acc + bdw_ref[:, cols]

    y = y_ref[...]
    mu = jnp.mean(y, axis=-1, keepdims=True)
    yc = y - mu
    z = yc * lax.rsqrt(jnp.mean(yc * yc, axis=-1, keepdims=True) + EPS) * lg_ref[...] + lb_ref[...]
    o_ref[...] = z * _sigmoid(z)


def conv_ln(buf, u, wdw, bdw, ln_g, ln_b, *, bb, tt, n_t=1):
    b, _, d = buf.shape
    aligned = tt % TILE_ROWS == 0
    kern = functools.partial(_conv_kernel, tt=tt, n_t=n_t, col_chunk=128 if aligned else 256)
    full_rows = CONV_PAD + tt + (TILE_ROWS if aligned else 0)
    vec = lambda i, j: (0, 0)
    tile = lambda i, j: (i * n_t + j, 0, 0)
    halo = pl.BlockSpec((bb, CONV_HALO, d), lambda i, j: (i, 0, 0))
    return pl.pallas_call(
        kern,
        out_shape=(jax.ShapeDtypeStruct((b * n_t, tt, d), F32), jax.ShapeDtypeStruct((b, CONV_HALO, d), F32)),
        grid=(b // bb, n_t),
        in_specs=[
            halo,
            pl.BlockSpec((bb, tt, d), tile),
            pl.BlockSpec((CONV_WIDTH, d), vec),
            pl.BlockSpec((1, d), vec),
            pl.BlockSpec((1, d), vec),
            pl.BlockSpec((1, d), vec),
        ],
        out_specs=(pl.BlockSpec((bb, tt, d), tile), halo),
        scratch_shapes=[pltpu.VMEM((bb, full_rows, d), F32), pltpu.VMEM((bb, tt, d), F32)],
        compiler_params=_params("parallel", "arbitrary"),
        name="conv_ln",
    )(buf, u, wdw, bdw.reshape(1, d), ln_g.reshape(1, d), ln_b.reshape(1, d))


def _pos_term_kernel(pe_ref, w_ref, o_ref):
    o_ref[0] = _dot(pe_ref[0], w_ref[0])


def pos_term(pe, w1):
    n = pe.shape[0]
    k = CMP_LEN * HEAD_DIM
    pe8 = jnp.broadcast_to(pe.reshape(n, 1, k), (n, 8, k)).astype(BF)
    return pl.pallas_call(
        _pos_term_kernel,
        out_shape=jax.ShapeDtypeStruct((n, 8, CMP_HIDDEN), F32),
        grid=(n,),
        in_specs=[pl.BlockSpec((1, 8, k), lambda i: (i, 0, 0)),
                  pl.BlockSpec((1, k, CMP_HIDDEN), lambda i: (i, 0, 0))],
        out_specs=pl.BlockSpec((1, 8, CMP_HIDDEN), lambda i: (i, 0, 0)),
        compiler_params=_params("parallel"),
        name="pos_term",
    )(pe8, w1.reshape(n, k, CMP_HIDDEN).astype(BF))


def _chunk_matrix(x, g):
    cols = [x[:, s * KV_COLS + g * HEAD_DIM: s * KV_COLS + (g + 1) * HEAD_DIM] for s in range(CMP_STRIDE)]
    return jnp.concatenate(cols, axis=1)


TILES_PER_CHUNK = CMP_STRIDE * KV_HEADS // TILE_ROWS
assert 2 * KV_HEADS == TILE_ROWS


def _page_chunk_rows(tile_ref):
    n_chunks = tile_ref.shape[1] // TILES_PER_CHUNK
    low = lax.broadcasted_iota(jnp.int32, (TILE_ROWS, HEAD_DIM), 0) < KV_HEADS
    cols = []
    for s in range(CMP_STRIDE):
        pieces = []
        for c in range(0, n_chunks, 2):
            a = tile_ref[0, TILES_PER_CHUNK * c + s // 2]
            b = tile_ref[0, TILES_PER_CHUNK * (c + 1) + s // 2]
            if s % 2 == 0:
                pieces.append(jnp.where(low, a, pltpu.roll(b, KV_HEADS, axis=0)))
            else:
                pieces.append(jnp.where(low, pltpu.roll(a, KV_HEADS, axis=0), b))
        cols.append(jnp.concatenate(pieces, axis=0))
    return jnp.concatenate(cols, axis=1)


def _compress_mlp(x, shift, w1_ref, pt_ref, w2_ref, b2_ref):
    n = x.shape[0]
    fs = _dot(x, w1_ref[0])
    first = fs[:, :CMP_HIDDEN]
    second = pltpu.roll(fs[:, CMP_HIDDEN:], n - shift, axis=0)
    pre = first + second + pt_ref[0, 0:1, :]
    hid = (pre * _sigmoid(pre)).astype(BF)
    out = _dot(hid, w2_ref[0]) + b2_ref[0]
    row = lax.broadcasted_iota(jnp.int32, (n, 1), 0)
    return jnp.where(row < n - shift, out, 0.0)


def _compress_rows_kernel(k_ref, v_ref, w1_ref, pt_ref, w2_ref, b2_ref, ok_ref, ov_ref):
    for i, (x_ref, o_ref) in enumerate(((k_ref, ok_ref), (v_ref, ov_ref))):
        w = (w1_ref.at[i:i + 1], pt_ref.at[i:i + 1], w2_ref.at[i:i + 1], b2_ref.at[i:i + 1])
        x = x_ref[0]
        for g in range(KV_HEADS):
            out = _compress_mlp(_chunk_matrix(x, g), 1, *w)
            o_ref[0, :, g * HEAD_DIM:(g + 1) * HEAD_DIM] = out.astype(o_ref.dtype)


def _compress_pages_kernel(*refs, n_pages):
    refs = refs[1:]
    w1_ref, pt_ref, w2_ref, b2_ref, ok_ref, ov_ref = refs[2 * n_pages:]
    for i, o_ref in enumerate((ok_ref, ov_ref)):
        w = (w1_ref.at[i:i + 1], pt_ref.at[i:i + 1], w2_ref.at[i:i + 1], b2_ref.at[i:i + 1])
        pages = refs[i * n_pages:(i + 1) * n_pages]
        x = jnp.concatenate([_page_chunk_rows(p).astype(BF) for p in pages], axis=0)
        o_ref[0] = _compress_mlp(x, KV_HEADS, *w).astype(o_ref.dtype)


def _compress_weight_specs():
    fixed3 = lambda *a: (0, 0, 0)
    return [
        pl.BlockSpec((2, CMP_STRIDE * HEAD_DIM, 2 * CMP_HIDDEN), fixed3),
        pl.BlockSpec((2, 8, CMP_HIDDEN), fixed3),
        pl.BlockSpec((2, CMP_HIDDEN, HEAD_DIM), fixed3),
        pl.BlockSpec((2, 1, HEAD_DIM), fixed3),
    ]


def compress_rows(k_rows, v_rows, cw):
    b, n, width = k_rows.shape
    spec = pl.BlockSpec((1, n, width), lambda i: (i, 0, 0))
    out = jax.ShapeDtypeStruct((b, n, KV_COLS), BF)
    ospec = pl.BlockSpec((1, n, KV_COLS), lambda i: (i, 0, 0))
    return pl.pallas_call(
        _compress_rows_kernel,
        out_shape=(out, out),
        grid=(b,),
        in_specs=[spec, spec] + _compress_weight_specs(),
        out_specs=(ospec, ospec),
        compiler_params=_params("parallel"),
        name="compress_rows",
    )(k_rows, v_rows, *cw)


def compress_pages(page_table, n_pages, cache_k, cache_v, cw):
    n_pool, tiles, rows, hd = cache_k.shape
    b = page_table.shape[0] // n_pages
    n = n_pages * tiles // TILES_PER_CHUNK * KV_HEADS

    def page_spec(p):
        return pl.BlockSpec((1, tiles, rows, hd), lambda i, pt: (pt[i * n_pages + p], 0, 0, 0))

    pages = [page_spec(p) for p in range(n_pages)]
    out = jax.ShapeDtypeStruct((b, n, HEAD_DIM), BF)
    ospec = pl.BlockSpec((1, n, HEAD_DIM), lambda i, pt: (i, 0, 0))
    grid_spec = pltpu.PrefetchScalarGridSpec(
        num_scalar_prefetch=1,
        grid=(b,),
        in_specs=pages + pages + _compress_weight_specs(),
        out_specs=(ospec, ospec),
    )
    return pl.pallas_call(
        functools.partial(_compress_pages_kernel, n_pages=n_pages),
        out_shape=(out, out),
        grid_spec=grid_spec,
        compiler_params=_params("parallel"),
        name="compress_pages",
    )(page_table, *([cache_k] * n_pages), *([cache_v] * n_pages), *cw)


def _softmax_cols(s, mask):
    s = jnp.where(mask, s, NEG_INF)
    m = jnp.max(s, axis=0, keepdims=True)
    m = jnp.where(m == NEG_INF, 0.0, m)
    p = jnp.where(mask, jnp.exp(s - m), 0.0)
    return p / jnp.maximum(jnp.sum(p, axis=0, keepdims=True), 1e-30)


def _top_rank_lanes(score, valid, n):
    idx = lax.broadcasted_iota(jnp.int32, score.shape, 1)
    cnt = jnp.zeros(score.shape, F32)
    for j in range(n):
        other = score[:, j:j + 1]
        tie = jnp.where(idx > j, 1.0, 0.0)
        cnt = cnt + jnp.where(other > score, 1.0, jnp.where(other == score, tie, 0.0))
    return jnp.where(cnt < TOP_N, jnp.where(valid, 1.0, 0.0), 0.0)


def _top_rank_sublanes(score, valid):
    n = score.shape[0]
    groups = [score[v:v + 8, :] for v in range(0, n, 8)]
    cnts = [jnp.zeros(g.shape, F32) for g in groups]
    for j in range(n):
        other = score[j:j + 1, :]
        for v, g in enumerate(groups):
            if 8 * v > j:
                beats = other >= g
            elif 8 * v + 7 <= j:
                beats = other > g
            else:
                sub = 8 * v + lax.broadcasted_iota(jnp.int32, g.shape, 0)
                beats = (other > g) | ((other == g) & (sub > j))
            cnts[v] = cnts[v] + jnp.where(beats, 1.0, 0.0)
    cnt = jnp.concatenate(cnts, axis=0)
    return jnp.where(cnt < TOP_N, jnp.where(valid, 1.0, 0.0), 0.0)


def _flash_update(s, vt, carry):
    m, l, acc = carry
    m_new = jnp.maximum(m, jnp.max(s, axis=0, keepdims=True))
    m_safe = jnp.where(m_new == NEG_INF, 0.0, m_new)
    alpha = jnp.exp(m - m_safe)
    p = jnp.exp(s - m_safe)
    l = alpha * l + jnp.sum(p, axis=0, keepdims=True)
    acc = alpha * acc + _dot(vt, p.astype(BF))
    return m_new, l, acc


def _prompt_attn_kernel(q_ref, gt_ref, kc_ref, vct_ref, ks_ref, vst_ref, kw_ref, vwt_ref,
                        o_ref, sel_ref, *, n_cmp):
    qt = pl.program_id(2)
    n_rows = GROUP * Q_TILE
    q = q_ref[0]
    q_rows = jnp.concatenate([q[:, r * HEAD_DIM:(r + 1) * HEAD_DIM] for r in range(GROUP)], axis=0)
    t0 = qt * Q_TILE
    pos_t = t0 + lax.broadcasted_iota(jnp.int32, (1, Q_TILE), 1)
    pos_rows = jnp.concatenate([pos_t] * GROUP, axis=1)

    n_cpad = kc_ref.shape[1]
    ci = lax.broadcasted_iota(jnp.int32, (n_cpad, 1), 0)
    cmask = (ci * CMP_STRIDE + (CMP_LEN - 1) <= pos_rows) & (ci < n_cmp)
    p_cmp = _softmax_cols(_dot_nt(kc_ref[0], q_rows), cmask)
    o_cmp = _dot(vct_ref[0, 0], p_cmp.astype(BF))

    n_steps = ks_ref.shape[2]
    blk_per_step = KEY_STEP // SEL_BLOCK
    n_sel = n_steps * blk_per_step
    p_sum = p_cmp[:, 0:Q_TILE]
    for r in range(1, GROUP):
        p_sum = p_sum + p_cmp[:, r * Q_TILE:(r + 1) * Q_TILE]
    bj = lax.broadcasted_iota(jnp.int32, (n_sel, n_cpad), 0) * SEL_BLOCK
    bi = lax.broadcasted_iota(jnp.int32, (n_sel, n_cpad), 1) * CMP_STRIDE
    overlap_t = jnp.where((bi < bj + SEL_BLOCK) & (bi + CMP_LEN > bj), 1.0, 0.0).astype(BF)
    hi, mid, lo = _split3(p_sum)
    imp = _dot(overlap_t, hi) + _dot(overlap_t, mid) + _dot(overlap_t, lo)
    blk = lax.broadcasted_iota(jnp.int32, (n_sel, 1), 0)
    cur = pos_t >> SEL_SHIFT
    forced = (blk == 0) | (blk == cur) | (blk == cur - 1)
    causal = blk * SEL_BLOCK <= pos_t
    score = jnp.where(causal, imp + FORCE_BONUS * jnp.where(forced, 1.0, 0.0), NEG_INF)
    sel = _top_rank_sublanes(score, causal)
    for st in range(n_steps):
        sel_ref[st] = sel[st * blk_per_step:(st + 1) * blk_per_step, :]

    init = (jnp.full((1, n_rows), NEG_INF, F32), jnp.zeros((1, n_rows), F32),
            jnp.zeros((HEAD_DIM, n_rows), F32))

    key_in_blk = lax.broadcasted_iota(jnp.int32, (SEL_BLOCK, 1), 0)

    def sel_scores(st):
        picked = sel_ref[st]
        pieces = []
        for h in range(blk_per_step):
            kpos = st * KEY_STEP + h * SEL_BLOCK + key_in_blk
            ok = (kpos <= pos_t) & (picked[h:h + 1, :] > 0.5)
            pieces.append(jnp.where(ok, 0.0, NEG_INF))
        b = jnp.concatenate(pieces, axis=0)
        return _dot_nt(ks_ref[0, 0, st], q_rows) + jnp.concatenate([b] * GROUP, axis=1)

    def sel_step(st, carry):
        return _flash_update(sel_scores(st), vst_ref[0, 0, st], carry)

    n_needed = (t0 + Q_TILE + KEY_STEP - 1) // KEY_STEP
    _, l_sel, acc_sel = lax.fori_loop(0, n_needed, sel_step, init)
    o_sel = acc_sel / jnp.maximum(l_sel, 1e-30)

    n_wt = (WINDOW + Q_TILE) // KEY_TILE
    first = qt * (Q_TILE // KEY_TILE) - WINDOW // KEY_TILE
    tiles = [jnp.maximum(first + i, 0) for i in range(n_wt)]
    k_win = jnp.concatenate([kw_ref[0, 0, kt] for kt in tiles], axis=0)
    vt_win = jnp.concatenate([vwt_ref[0, 0, kt] for kt in tiles], axis=1)
    kpos = first * KEY_TILE + lax.broadcasted_iota(jnp.int32, (n_wt * KEY_TILE, 1), 0)
    ok = (kpos >= 0) & (kpos <= pos_rows) & (pos_rows - kpos < WINDOW)
    s_win = _dot_nt(k_win, q_rows) + jnp.where(ok, 0.0, NEG_INF)
    _, l_win, acc_win = _flash_update(s_win, vt_win, init)
    o_win = acc_win / jnp.maximum(l_win, 1e-30)

    gt = gt_ref[0]
    for r in range(GROUP):
        cols = slice(r * Q_TILE, (r + 1) * Q_TILE)
        base = r * N_BRANCH
        mix = (gt[base:base + 1, :] * o_cmp[:, cols] + gt[base + 1:base + 2, :] * o_sel[:, cols]
               + gt[base + 2:base + 3, :] * o_win[:, cols])
        o_ref[0, :, r * HEAD_DIM:(r + 1) * HEAD_DIM] = mix.T.astype(o_ref.dtype)


def prompt_attention(q, gates_t, kc, vct, ks, vst, kw, vwt, n_cmp):
    b = ks.shape[0]
    t = ks.shape[2] * KEY_STEP
    q = q.reshape(-1, Q_TILE, q.shape[-1])
    n_q = t // Q_TILE
    n_steps = ks.shape[2]
    n_kt = kw.shape[2]
    n_cpad = kc.shape[1]
    whole = lambda i, g, j: (i, g, 0, 0, 0)
    width = GROUP * HEAD_DIM
    return pl.pallas_call(
        functools.partial(_prompt_attn_kernel, n_cmp=n_cmp),
        out_shape=jax.ShapeDtypeStruct((b, t, N_HEADS * HEAD_DIM), BF),
        grid=(b, KV_HEADS, n_q),
        in_specs=[
            pl.BlockSpec((1, Q_TILE, width), lambda i, g, j: (i * n_q + j, 0, g)),
            pl.BlockSpec((1, GROUP * N_BRANCH, Q_TILE), lambda i, g, j: (g, 0, i * n_q + j)),
            pl.BlockSpec((1, n_cpad, HEAD_DIM), lambda i, g, j: (i, 0, g)),
            pl.BlockSpec((1, 1, HEAD_DIM, n_cpad), lambda i, g, j: (i, g, 0, 0)),
            pl.BlockSpec((1, 1, n_steps, KEY_STEP, HEAD_DIM), whole),
            pl.BlockSpec((1, 1, n_steps, HEAD_DIM, KEY_STEP), whole),
            pl.BlockSpec((1, 1, n_kt, KEY_TILE, HEAD_DIM), whole),
            pl.BlockSpec((1, 1, n_kt, HEAD_DIM, KEY_TILE), whole),
        ],
        out_specs=pl.BlockSpec((1, Q_TILE, width), lambda i, g, j: (i, j, g)),
        scratch_shapes=[pltpu.VMEM((n_steps, KEY_STEP // SEL_BLOCK, Q_TILE), F32)],
        compiler_params=_params("parallel", "parallel", "arbitrary"),
        name="prompt_attention",
    )(q, gates_t, kc, vct, ks, vst, kw, vwt)


def _softmax_two(s_a, bias_a, s_b, bias_b):
    s_a = s_a + bias_a
    s_b = s_b + bias_b
    m = jnp.maximum(jnp.max(s_a, axis=1, keepdims=True), jnp.max(s_b, axis=1, keepdims=True))
    m = jnp.where(m == NEG_INF, 0.0, m)
    p_a = jnp.exp(s_a - m)
    p_b = jnp.exp(s_b - m)
    total = jnp.sum(p_a, axis=1, keepdims=True) + jnp.sum(p_b, axis=1, keepdims=True)
    return p_a, p_b, 1.0 / jnp.maximum(total, 1e-30)


def _sample_attn_kernel(*refs, n_pages, t_new, past_len, n_cmp):
    refs = refs[1:]
    q_ref, gate_ref, kc_ref, vc_ref = refs[:4]
    ks_pages = refs[4:4 + n_pages]
    vs_pages = refs[4 + n_pages:4 + 2 * n_pages]
    (ksn_ref, vsn_ref, kwp_ref, vwp_ref, kwn_ref, vwn_ref, o_ref,
     expand_ref, past_bias_ref, win_bias_ref) = refs[4 + 2 * n_pages:]

    n_rows = q_ref.shape[1]
    rows_per_head = n_rows // KV_HEADS
    q = q_ref[0]
    row = lax.broadcasted_iota(jnp.int32, (n_rows, 1), 0)
    row_head = _div(row, rows_per_head)
    pos = past_len + _mod(row, t_new)
    n_past = n_pages * ks_pages[0].shape[1]
    n_win = kwp_ref.shape[1]

    def col_key(n_cols):
        return _div(lax.broadcasted_iota(jnp.int32, (1, n_cols), 1), KV_HEADS)

    def head_bias(n_cols):
        col = lax.broadcasted_iota(jnp.int32, (1, n_cols), 1)
        return jnp.where(_mod(col, KV_HEADS) == row_head, 0.0, NEG_INF)

    @pl.when(pl.program_id(0) == 0)
    def _():
        ej = lax.broadcasted_iota(jnp.int32, (128, n_past), 0)
        ek = _div(lax.broadcasted_iota(jnp.int32, (128, n_past), 1), KV_HEADS)
        expand_ref[...] = jnp.where((ek >> SEL_SHIFT) == ej, 1.0, 0.0).astype(BF)
        past_bias_ref[...] = jnp.where(col_key(n_past) <= pos, head_bias(n_past), NEG_INF)
        wpos = past_len - n_win // KV_HEADS + col_key(n_win)
        win_bias_ref[...] = jnp.where((wpos <= pos) & (pos - wpos < WINDOW), head_bias(n_win), NEG_INF)

    new_pos = past_len + col_key(128)
    new_bias = jnp.where((col_key(128) < t_new) & (new_pos <= pos), head_bias(128), NEG_INF)

    def pad_new(ref):
        x = ref[0]
        return jnp.concatenate([x, jnp.zeros((128 - x.shape[0], x.shape[1]), x.dtype)], axis=0).astype(BF)

    n_c = kc_ref.shape[1]
    chunk = col_key(n_c)
    c_ok = (chunk * CMP_STRIDE + (CMP_LEN - 1) <= pos) & (chunk < n_cmp)
    s_c = _dot_nt(q, kc_ref[0]) + jnp.where(c_ok, head_bias(n_c), NEG_INF)
    m_c = jnp.max(s_c, axis=1, keepdims=True)
    m_c = jnp.where(m_c == NEG_INF, 0.0, m_c)
    p_c = jnp.exp(s_c - m_c)
    p_c = p_c / jnp.maximum(jnp.sum(p_c, axis=1, keepdims=True), 1e-30)
    o_cmp = _dot(p_c.astype(BF), vc_ref[0])

    ra = lax.broadcasted_iota(jnp.int32, (n_rows, n_rows), 0)
    rb = lax.broadcasted_iota(jnp.int32, (n_rows, n_rows), 1)
    same_gt = ((_div(ra, rows_per_head) == _div(rb, rows_per_head))
               & (_mod(ra, t_new) == _mod(rb, t_new)))
    group_sum = jnp.where(same_gt, 1.0, 0.0).astype(BF)
    hi, mid, lo = _split3(p_c)
    p_sum = _dot(group_sum, hi) + _dot(group_sum, mid) + _dot(group_sum, lo)
    n_keys = past_len + t_new
    n_sel = -(-n_keys // SEL_BLOCK)
    bi = _div(lax.broadcasted_iota(jnp.int32, (n_c, 128), 0), KV_HEADS) * CMP_STRIDE
    bj = lax.broadcasted_iota(jnp.int32, (n_c, 128), 1) * SEL_BLOCK
    overlap = jnp.where((bi < bj + SEL_BLOCK) & (bi + CMP_LEN > bj), 1.0, 0.0).astype(BF)
    imp = _dot01(p_sum, overlap)
    blk = lax.broadcasted_iota(jnp.int32, (1, 128), 1)
    cur = pos >> SEL_SHIFT
    forced = (blk == 0) | (blk == cur) | (blk == cur - 1)
    causal = (blk * SEL_BLOCK <= pos) & (blk < n_sel)
    score = jnp.where(causal, imp + FORCE_BONUS * jnp.where(forced, 1.0, 0.0), NEG_INF)
    sel = _top_rank_lanes(score, causal, n_sel)

    k_past = jnp.concatenate([p[0].astype(BF) for p in ks_pages], axis=0)
    v_past = jnp.concatenate([p[0].astype(BF) for p in vs_pages], axis=0)
    sel16 = sel.astype(BF)
    bias_past = jnp.where(_dot(sel16, expand_ref[...]) > 0.5, past_bias_ref[...], NEG_INF)
    nj = lax.broadcasted_iota(jnp.int32, (128, 128), 0)
    nk = past_len + _div(lax.broadcasted_iota(jnp.int32, (128, 128), 1), KV_HEADS)
    expand_new = jnp.where((nk >> SEL_SHIFT) == nj, 1.0, 0.0).astype(BF)
    bias_new = jnp.where(_dot(sel16, expand_new) > 0.5, new_bias, NEG_INF)
    p_a, p_b, inv = _softmax_two(_dot_nt(q, k_past), bias_past, _dot_nt(q, pad_new(ksn_ref)), bias_new)
    o_sel = (_dot(p_a.astype(BF), v_past) + _dot(p_b.astype(BF), pad_new(vsn_ref))) * inv

    bias_wn = jnp.where(pos - new_pos < WINDOW, new_bias, NEG_INF)
    p_a, p_b, inv = _softmax_two(_dot_nt(q, kwp_ref[0].astype(BF)), win_bias_ref[...],
                                 _dot_nt(q, pad_new(kwn_ref)), bias_wn)
    o_win = (_dot(p_a.astype(BF), vwp_ref[0].astype(BF)) + _dot(p_b.astype(BF), pad_new(vwn_ref))) * inv

    gate = gate_ref[0]
    o = gate[:, 0:1] * o_cmp + gate[:, 1:2] * o_sel + gate[:, 2:3] * o_win
    o_ref[0] = o.astype(o_ref.dtype)


def sample_attention(page_table, q, gates, kc, vc, cache_ks, cache_vs, ks_new, vs_new,
                     kw_past, vw_past, kw_new, vw_new, *, t_new, n_cmp):
    b, n_rows, _ = q.shape
    n_pages = page_table.shape[0] // b
    page_rows = cache_ks.shape[1]
    n_c = kc.shape[1]
    n_win = kw_past.shape[1]
    n_new = ks_new.shape[1]
    n_past = n_pages * page_rows

    def page_spec(p):
        return pl.BlockSpec((1, page_rows, HEAD_DIM), lambda i, pt: (pt[i * n_pages + p], 0, 0))

    per_seq = lambda rows, cols=HEAD_DIM: pl.BlockSpec((1, rows, cols), lambda i, pt: (i, 0, 0))
    pages = [page_spec(p) for p in range(n_pages)]
    grid_spec = pltpu.PrefetchScalarGridSpec(
        num_scalar_prefetch=1,
        grid=(b,),
        in_specs=[per_seq(n_rows), per_seq(n_rows, N_BRANCH), per_seq(n_c), per_seq(n_c)]
        + pages + pages
        + [per_seq(n_new), per_seq(n_new), per_seq(n_win), per_seq(n_win), per_seq(n_new), per_seq(n_new)],
        out_specs=per_seq(n_rows),
        scratch_shapes=[pltpu.VMEM((128, n_past), BF), pltpu.VMEM((n_rows, n_past), F32),
                        pltpu.VMEM((n_rows, n_win), F32)],
    )
    kern = functools.partial(_sample_attn_kernel, n_pages=n_pages, t_new=t_new,
                             past_len=n_past // KV_HEADS, n_cmp=n_cmp)
    return pl.pallas_call(
        kern,
        out_shape=jax.ShapeDtypeStruct((b, n_rows, HEAD_DIM), BF),
        grid_spec=grid_spec,
        compiler_params=_params("arbitrary"),
        name="sample_attention",
    )(page_table, q, gates, kc, vc, *([cache_ks] * n_pages), *([cache_vs] * n_pages),
      ks_new, vs_new, kw_past, vw_past, kw_new, vw_new)


def _window_update_kernel(k_ref, v_ref, kn_ref, vn_ref, ok_ref, ov_ref):
    n = k_ref.shape[1]
    n_new = kn_ref.shape[1]
    for s_ref, n_ref, o_ref in ((k_ref, kn_ref, ok_ref), (v_ref, vn_ref, ov_ref)):
        o_ref[:, 0:n - n_new, :] = s_ref[:, n_new:n, :]
        o_ref[:, n - n_new:n, :] = n_ref[...]


def window_update(k_state, v_state, k_new, v_new):
    b, n, d = k_state.shape
    n_new = k_new.shape[1]
    bb = 2
    state = pl.BlockSpec((bb, n, d), lambda i: (i, 0, 0))
    new = pl.BlockSpec((bb, n_new, d), lambda i: (i, 0, 0))
    out = jax.ShapeDtypeStruct((b, n, d), F32)
    return pl.pallas_call(
        _window_update_kernel,
        out_shape=(out, out),
        grid=(b // bb,),
        in_specs=[state, state, new, new],
        out_specs=(state, state),
        compiler_params=_params("parallel"),
        name="window_update",
    )(k_state, v_state, k_new, v_new)


def _rope_tables(pos):
    half = HEAD_DIM // 2
    inv = ROPE_THETA ** (-jnp.arange(half, dtype=F32) / half)
    ang = pos.astype(F32)[:, None] * inv[None, :]
    cos = jnp.cos(ang)
    sin = jnp.sin(ang)
    return jnp.concatenate([cos, cos], axis=1), jnp.concatenate([-sin, sin], axis=1)


def kernel(x_prompt, x_sample, cache_k_cmp, cache_v_cmp, cache_k_sel, cache_v_sel, state_k_win, state_v_win, state_conv, page_table, norm_g, conv_w1, conv_b1, conv_wdw, conv_bdw, conv_ln_g, conv_ln_b, conv_w2, conv_b2, kv_norm_g, w_kv, cmp_w1, cmp_pe, cmp_w2, cmp_b2, nsa_w_qg, nsa_w_o, ffn_w_in, ffn_w_out):
    bp, tp, d = x_prompt.shape
    bs, ts, _ = x_sample.shape
    n_p = bp * tp
    n_s = bs * ts
    n_pool, page, _, _ = cache_k_sel.shape
    n_pages = page_table.shape[1]
    past_len = n_pages * page
    bf = lambda w: w.astype(BF)

    def split(a):
        return a[:n_p], a[n_p:]

    x_in = (x_prompt.reshape(n_p, d), x_sample.reshape(n_s, d))
    pos = jnp.concatenate([jnp.tile(jnp.arange(tp, dtype=jnp.int32), bp),
                           jnp.tile(past_len + jnp.arange(ts, dtype=jnp.int32), bs)])
    cos, sin = _rope_tables(pos)

    g = norm_g[0]
    u = glu_in(x_in, g[0], bf(conv_w1[0]), conv_b1[0])
    conv_tile = 128
    assert tp % conv_tile == 0 and (n_p + n_s) % conv_tile == 0
    u_s = u[n_p:].reshape(bs, ts, d)
    conv_args = (conv_wdw[0], conv_bdw[0], conv_ln_g[0], conv_ln_b[0])
    v_p, conv_p = conv_ln(jnp.zeros((bp, CONV_HALO, d), F32), u.reshape(-1, conv_tile, d), *conv_args,
                          bb=1, tt=conv_tile, n_t=tp // conv_tile)
    v_s, conv_s = conv_ln(state_conv[0], u_s, *conv_args, bb=16, tt=ts)
    x = mm_norm_res((v_p.reshape(n_p, d), v_s.reshape(n_s, d)), bf(conv_w2[0]), conv_b2[0], g[1], x_in)
    ffn_in16, ffn_out16 = bf(ffn_w_in), bf(ffn_w_out)
    x = ffn(x, g[2], ffn_in16, ffn_out16, g[3], 0)
    conv_p, conv_s = conv_p[None], conv_s[None]

    kv32, kv16 = kv_proj(x, kv_norm_g, bf(w_kv), cos, sin)
    rows_p = [kv32[s, :n_p].reshape(bp, tp, KV_HEADS, HEAD_DIM) for s in range(N_KV_SLOTS)]
    rows_s = [kv32[s, n_p:].reshape(bs, ts, KV_HEADS, HEAD_DIM) for s in range(N_KV_SLOTS)]
    n_wp = min(WINDOW, tp)
    kw_p, vw_p = rows_p[4][:, tp - n_wp:], rows_p[5][:, tp - n_wp:]

    def new_rows(slot):
        return kv32[slot, n_p:].reshape(bs, ts * KV_HEADS, HEAD_DIM)

    head_rows = lambda a: a.reshape(a.shape[0], -1, HEAD_DIM)
    kw_s, vw_s = window_update(head_rows(state_k_win), head_rows(state_v_win), new_rows(4), new_rows(5))
    kw_s, vw_s = kw_s.reshape(state_k_win.shape), vw_s.reshape(state_v_win.shape)

    w1 = cmp_w1.reshape(2, 2, CMP_STRIDE * HEAD_DIM, CMP_HIDDEN)
    w1cat = bf(jnp.concatenate([w1[:, 0], w1[:, 1]], axis=-1))
    cw = (w1cat, pos_term(cmp_pe, cmp_w1), bf(cmp_w2), cmp_b2.reshape(2, 1, HEAD_DIM))
    chunk_cols = CMP_STRIDE * KV_COLS
    kc_p, vc_p = compress_rows(kv16[0, :n_p].reshape(bp, tp // CMP_STRIDE, chunk_cols),
                               kv16[1, :n_p].reshape(bp, tp // CMP_STRIDE, chunk_cols), cw)
    pt_flat = page_table.reshape(-1)
    assert (past_len + ts) // CMP_STRIDE == past_len // CMP_STRIDE
    page_rows = page * KV_HEADS
    as_tiles = lambda c: c.reshape(n_pool, page_rows // TILE_ROWS, TILE_ROWS, HEAD_DIM)
    kc_s, vc_s = compress_pages(pt_flat, n_pages, as_tiles(cache_k_cmp), as_tiles(cache_v_cmp), cw)

    g = norm_g[1]
    w_qg = nsa_w_qg[0]
    n_q_cols = N_HEADS * HEAD_DIM
    n_gate = N_HEADS * N_BRANCH
    wg = w_qg[:, n_q_cols:]
    wg_pad = bf(jnp.pad(wg, ((0, 0), (0, 128 - n_gate))))
    q, gates, gates_t = q_proj(x, g[0], bf(w_qg[:, :n_q_cols]), wg_pad, bf(wg.T), cos, sin)

    assert tp % KEY_STEP == 0 and tp % Q_TILE == 0

    def key_tiles(a, tk):
        return a.reshape(bp, tp // tk, tk, KV_HEADS, HEAD_DIM).transpose(0, 3, 1, 2, 4)

    def value_tiles_t(a, tk):
        return a.reshape(bp, tp // tk, tk, KV_HEADS, HEAD_DIM).transpose(0, 3, 1, 4, 2)

    n_cmp_p = tp // CMP_STRIDE - 1
    vct_p = vc_p.reshape(bp, tp // CMP_STRIDE, KV_HEADS, HEAD_DIM).transpose(0, 2, 3, 1)
    gates_t_p = gates_t[:, :n_p].reshape(KV_HEADS, GROUP * N_BRANCH, n_p)
    assert (n_p + n_s) % Q_TILE == 0
    o_p = prompt_attention(q, gates_t_p, kc_p, vct_p,
                           key_tiles(kv16[2, :n_p], KEY_STEP), value_tiles_t(kv16[3, :n_p], KEY_STEP),
                           key_tiles(kv16[4, :n_p], KEY_TILE), value_tiles_t(kv16[5, :n_p], KEY_TILE), n_cmp_p)

    def seq_rows(a, width):
        return a.reshape(bs, ts, KV_HEADS, GROUP, width).transpose(0, 2, 3, 1, 4).reshape(bs, -1, width)

    q_s = seq_rows(q[n_p:], HEAD_DIM)
    gates_s = seq_rows(gates[n_p:, :n_gate], N_BRANCH)

    o_s = sample_attention(
        pt_flat, q_s, gates_s, kc_s, vc_s,
        head_rows(cache_k_sel), head_rows(cache_v_sel), new_rows(2), new_rows(3),
        head_rows(state_k_win), head_rows(state_v_win), new_rows(4), new_rows(5), t_new=ts, n_cmp=past_len // CMP_STRIDE - 1)
    o_s = o_s.reshape(bs, KV_HEADS, GROUP, ts, HEAD_DIM).transpose(0, 3, 1, 2, 4).reshape(n_s, n_q_cols)
    x = mm_norm_res((o_p.reshape(n_p, n_q_cols), o_s), bf(nsa_w_o[0]), None, g[1], x)
    x = ffn(x, g[2], ffn_in16, ffn_out16, g[3], 1)

    y_p, y_s = split(x)
    return (y_p.reshape(bp, tp, d), y_s.reshape(bs, ts, d), conv_p,
            rows_p[0], rows_p[1], rows_p[2], rows_p[3], kw_p, vw_p,
            conv_s, rows_s[0], rows_s[1], rows_s[2], rows_s[3], kw_s, vw_s)
```

```python
import functools

import jax
import jax.numpy as jnp
from jax import lax
from jax.experimental import pallas as pl
from jax.experimental.pallas import tpu as pltpu

D_MODEL = 2048
CONV_WIDTH = 31
CONV_HALO = CONV_WIDTH - 1
N_HEADS = 16
HEAD_DIM = 128
KV_HEADS = 4
GROUP = N_HEADS // KV_HEADS
N_BRANCH = 3
N_KV_SLOTS = 6
KV_COLS = KV_HEADS * HEAD_DIM
CMP_LEN = 32
CMP_STRIDE = 16
CMP_HIDDEN = 256
SEL_BLOCK = 64
TOP_N = 16
WINDOW = 512
ROPE_THETA = 10000.0
FORCE_BONUS = 1.0e4
EPS = 1e-6
SCALE = HEAD_DIM ** -0.5
Q_SCALE = SCALE * 1.4426950408889634
SEL_SHIFT = SEL_BLOCK.bit_length() - 1
assert 1 << SEL_SHIFT == SEL_BLOCK

TILE_ROWS = 8
BF = jnp.bfloat16
F32 = jnp.float32
NEG_INF = float("-inf")

VMEM_LIMIT_BYTES = 56 * 1024 * 1024
ROW_TILE = 512
KEY_TILE = 128
KEY_STEP = 1024
Q_TILE = 256


def _params(*sem):
    return pltpu.CompilerParams(dimension_semantics=sem, vmem_limit_bytes=VMEM_LIMIT_BYTES)


def _sigmoid(x):
    return 1.0 / (1.0 + jnp.exp(-x))


def _rms(x, g):
    return x * lax.rsqrt(jnp.mean(x * x, axis=-1, keepdims=True) + EPS) * g


def _rope(y, cos, sin_signed):
    return y * cos + pltpu.roll(y, HEAD_DIM // 2, axis=1) * sin_signed


def _split3(x):
    hi = x.astype(BF)
    r1 = x - hi.astype(F32)
    mid = r1.astype(BF)
    lo = (r1 - mid.astype(F32)).astype(BF)
    return hi, mid, lo


def _div(x, n):
    assert n & (n - 1) == 0
    return x >> (n.bit_length() - 1)


def _mod(x, n):
    assert n & (n - 1) == 0
    return x & (n - 1)


def _dot(a, b):
    return jnp.dot(a, b, preferred_element_type=F32)


def _dot_nt(a, b):
    return lax.dot_general(a, b, (((1,), (1,)), ((), ())), preferred_element_type=F32)


def _dot01(x, onehot):
    hi, mid, lo = _split3(x)
    return _dot(hi, onehot) + _dot(mid, onehot) + _dot(lo, onehot)


def _row_pair_specs(n_first, block):
    first = pl.BlockSpec(block, lambda i, *_: (jnp.minimum(i, n_first - 1), 0))
    second = pl.BlockSpec(block, lambda i, *_: (jnp.maximum(i - n_first, 0), 0))
    return [first, second]


def _glu_in_kernel(xa_ref, xb_ref, g_ref, wa_ref, wb_ref, ba_ref, bb_ref, o_ref, xn_ref, *, n_first):
    i = pl.program_id(0)
    j = pl.program_id(1)

    @pl.when((j == 0) & (i < n_first))
    def _():
        xn_ref[...] = _rms(xa_ref[...], g_ref[...]).astype(BF)

    @pl.when((j == 0) & (i >= n_first))
    def _():
        xn_ref[...] = _rms(xb_ref[...], g_ref[...]).astype(BF)

    xn = xn_ref[...]
    a = _dot(xn, wa_ref[...]) + ba_ref[...]
    b = _dot(xn, wb_ref[...]) + bb_ref[...]
    o_ref[...] = a * _sigmoid(b)


def glu_in(x_pair, g, w, b):
    xa, xb = x_pair
    d = xa.shape[1]
    m = xa.shape[0] + xb.shape[0]
    n_first = xa.shape[0] // ROW_TILE
    tn = 512
    nj = d // tn
    return pl.pallas_call(
        functools.partial(_glu_in_kernel, n_first=n_first),
        out_shape=jax.ShapeDtypeStruct((m, d), F32),
        grid=(m // ROW_TILE, nj),
        in_specs=_row_pair_specs(n_first, (ROW_TILE, d)) + [
            pl.BlockSpec((1, d), lambda i, j: (0, 0)),
            pl.BlockSpec((d, tn), lambda i, j: (0, j)),
            pl.BlockSpec((d, tn), lambda i, j: (0, j + nj)),
            pl.BlockSpec((1, tn), lambda i, j: (0, j)),
            pl.BlockSpec((1, tn), lambda i, j: (0, j + nj)),
        ],
        out_specs=pl.BlockSpec((ROW_TILE, tn), lambda i, j: (i, j)),
        scratch_shapes=[pltpu.VMEM((ROW_TILE, d), BF)],
        compiler_params=_params("parallel", "arbitrary"),
        name="glu_in",
    )(xa, xb, g.reshape(1, d), w, w, b.reshape(1, 2 * d), b.reshape(1, 2 * d))


def _mm_norm_res_kernel(*refs, has_bias, x_is_pair, n_first):
    ha_ref, hb_ref, w_ref = refs[:3]
    refs = refs[3:]
    if has_bias:
        b_ref, refs = refs[0], refs[1:]
    g_ref, refs = refs[0], refs[1:]
    x_refs, (o_ref, h_ref) = refs[:-2], refs[-2:]
    is_first = pl.program_id(0) < n_first

    @pl.when(is_first)
    def _():
        h_ref[...] = ha_ref[...].astype(BF)

    @pl.when(jnp.logical_not(is_first))
    def _():
        h_ref[...] = hb_ref[...].astype(BF)

    y = _dot(h_ref[...], w_ref[...])
    if has_bias:
        y = y + b_ref[...]
    r = _rms(y, g_ref[...])
    if x_is_pair:
        @pl.when(is_first)
        def _():
            o_ref[...] = x_refs[0][...] + r

        @pl.when(jnp.logical_not(is_first))
        def _():
            o_ref[...] = x_refs[1][...] + r
    else:
        o_ref[...] = x_refs[0][...] + r


def mm_norm_res(h_pair, w, b, g, x):
    ha, hb = h_pair
    k = ha.shape[1]
    m = ha.shape[0] + hb.shape[0]
    d = w.shape[1]
    tm = 256
    n_first = ha.shape[0] // tm
    row = lambda i: (i, 0)
    fixed = lambda i: (0, 0)
    in_specs = _row_pair_specs(n_first, (tm, k)) + [pl.BlockSpec((k, d), fixed)]
    args = [ha, hb, w]
    if b is not None:
        in_specs.append(pl.BlockSpec((1, d), fixed))
        args.append(b.reshape(1, d))
    in_specs.append(pl.BlockSpec((1, d), fixed))
    args.append(g.reshape(1, d))
    x_is_pair = isinstance(x, tuple)
    if x_is_pair:
        in_specs += _row_pair_specs(n_first, (tm, d))
        args += list(x)
    else:
        in_specs.append(pl.BlockSpec((tm, d), row))
        args.append(x)
    return pl.pallas_call(
        functools.partial(_mm_norm_res_kernel, has_bias=b is not None, x_is_pair=x_is_pair, n_first=n_first),
        out_shape=jax.ShapeDtypeStruct((m, d), F32),
        grid=(m // tm,),
        in_specs=in_specs,
        out_specs=pl.BlockSpec((tm, d), row),
        scratch_shapes=[pltpu.VMEM((tm, k), BF)],
        compiler_params=_params("parallel"),
        name="mm_norm_res",
    )(*args)


def _ffn_kernel(x_ref, gi_ref, wa_ref, wb_ref, wo_ref, go_ref, o_ref, xn_ref, acc_ref):
    j = pl.program_id(1)

    @pl.when(j == 0)
    def _():
        xn_ref[...] = _rms(x_ref[...], gi_ref[...]).astype(BF)
        acc_ref[...] = jnp.zeros_like(acc_ref)

    xn = xn_ref[...]
    a = _dot(xn, wa_ref[0])
    b = _dot(xn, wb_ref[0])
    h = (a * _sigmoid(a) * b).astype(BF)
    acc_ref[...] += _dot(h, wo_ref[0])

    @pl.when(j == pl.num_programs(1) - 1)
    def _():
        o_ref[...] = x_ref[...] + _rms(acc_ref[...], go_ref[...])


def ffn(x, g_in, w_in, w_out, g_out, layer):
    m, d = x.shape
    f = w_out.shape[1]
    tf = 512
    nj = f // tf
    return pl.pallas_call(
        _ffn_kernel,
        out_shape=jax.ShapeDtypeStruct((m, d), F32),
        grid=(m // ROW_TILE, nj),
        in_specs=[
            pl.BlockSpec((ROW_TILE, d), lambda i, j: (i, 0)),
            pl.BlockSpec((1, d), lambda i, j: (0, 0)),
            pl.BlockSpec((1, d, tf), lambda i, j: (layer, 0, j)),
            pl.BlockSpec((1, d, tf), lambda i, j: (layer, 0, j + nj)),
            pl.BlockSpec((1, tf, d), lambda i, j: (layer, j, 0)),
            pl.BlockSpec((1, d), lambda i, j: (0, 0)),
        ],
        out_specs=pl.BlockSpec((ROW_TILE, d), lambda i, j: (i, 0)),
        scratch_shapes=[pltpu.VMEM((ROW_TILE, d), BF), pltpu.VMEM((ROW_TILE, d), F32)],
        compiler_params=_params("parallel", "arbitrary"),
        name="ffn",
    )(x, g_in.reshape(1, d), w_in, w_in, w_out, g_out.reshape(1, d))


def _kv_proj_kernel(x_ref, g_ref, w_ref, cos_ref, sin_ref, o32_ref, o16_ref, xn_ref):
    s = pl.program_id(1)

    @pl.when(s == 0)
    def _():
        xn_ref[...] = _rms(x_ref[...], g_ref[...]).astype(BF)

    y = _dot(xn_ref[...], w_ref[...])

    @pl.when(s % 2 == 0)
    def _():
        cos = cos_ref[...]
        sin = sin_ref[...]
        for h in range(KV_HEADS):
            cols = slice(h * HEAD_DIM, (h + 1) * HEAD_DIM)
            r = _rope(y[:, cols], cos, sin)
            o32_ref[0, :, h, :] = r
            o16_ref[0, :, cols] = r.astype(BF)

    @pl.when(s % 2 == 1)
    def _():
        for h in range(KV_HEADS):
            o32_ref[0, :, h, :] = y[:, h * HEAD_DIM:(h + 1) * HEAD_DIM]
        o16_ref[0] = y.astype(BF)


def kv_proj(x, g, w, cos, sin):
    m, d = x.shape
    out = jax.ShapeDtypeStruct((N_KV_SLOTS, m, KV_HEADS, HEAD_DIM), F32)
    out16 = jax.ShapeDtypeStruct((N_KV_SLOTS, m, KV_COLS), BF)
    return pl.pallas_call(
        _kv_proj_kernel,
        out_shape=(out, out16),
        grid=(m // ROW_TILE, N_KV_SLOTS),
        in_specs=[
            pl.BlockSpec((ROW_TILE, d), lambda i, s: (i, 0)),
            pl.BlockSpec((1, d), lambda i, s: (0, 0)),
            pl.BlockSpec((d, KV_COLS), lambda i, s: (0, s)),
            pl.BlockSpec((ROW_TILE, HEAD_DIM), lambda i, s: (i, 0)),
            pl.BlockSpec((ROW_TILE, HEAD_DIM), lambda i, s: (i, 0)),
        ],
        out_specs=(pl.BlockSpec((1, ROW_TILE, KV_HEADS, HEAD_DIM), lambda i, s: (s, i, 0, 0)),
                   pl.BlockSpec((1, ROW_TILE, KV_COLS), lambda i, s: (s, i, 0))),
        scratch_shapes=[pltpu.VMEM((ROW_TILE, d), BF)],
        compiler_params=_params("parallel", "arbitrary"),
        name="kv_proj",
    )(x, g.reshape(1, d), w, cos, sin)


def _q_proj_kernel(x_ref, g_ref, wq_ref, wg_ref, wgt_ref, cos_ref, sin_ref,
                   q_ref, gate_ref, gatet_ref, xn_ref):
    @pl.when(pl.program_id(1) == 0)
    def _():
        xn = _rms(x_ref[...], g_ref[...]).astype(BF)
        xn_ref[...] = xn
        gate_ref[...] = _sigmoid(_dot(xn, wg_ref[...]))
        gatet_ref[...] = _sigmoid(_dot_nt(wgt_ref[...], xn))

    y = _dot(xn_ref[...], wq_ref[...])
    cos = cos_ref[...]
    sin = sin_ref[...]
    for h in range(GROUP):
        cols = slice(h * HEAD_DIM, (h + 1) * HEAD_DIM)
        q_ref[:, cols] = (_rope(y[:, cols], cos, sin) * Q_SCALE).astype(BF)


def q_proj(x, g, wq, wg_pad, wg_t, cos, sin):
    m, d = x.shape
    n_gate = wg_t.shape[0]
    tn = GROUP * HEAD_DIM
    return pl.pallas_call(
        _q_proj_kernel,
        out_shape=(jax.ShapeDtypeStruct((m, N_HEADS * HEAD_DIM), BF),
                   jax.ShapeDtypeStruct((m, 128), F32),
                   jax.ShapeDtypeStruct((n_gate, m), F32)),
        grid=(m // ROW_TILE, KV_HEADS),
        in_specs=[
            pl.BlockSpec((ROW_TILE, d), lambda i, j: (i, 0)),
            pl.BlockSpec((1, d), lambda i, j: (0, 0)),
            pl.BlockSpec((d, tn), lambda i, j: (0, j)),
            pl.BlockSpec((d, 128), lambda i, j: (0, 0)),
            pl.BlockSpec((n_gate, d), lambda i, j: (0, 0)),
            pl.BlockSpec((ROW_TILE, HEAD_DIM), lambda i, j: (i, 0)),
            pl.BlockSpec((ROW_TILE, HEAD_DIM), lambda i, j: (i, 0)),
        ],
        out_specs=(pl.BlockSpec((ROW_TILE, tn), lambda i, j: (i, j)),
                   pl.BlockSpec((ROW_TILE, 128), lambda i, j: (i, 0)),
                   pl.BlockSpec((n_gate, ROW_TILE), lambda i, j: (0, i))),
        scratch_shapes=[pltpu.VMEM((ROW_TILE, d), BF)],
        compiler_params=_params("parallel", "arbitrary"),
        name="q_proj",
    )(x, g.reshape(1, d), wq, wg_pad, wg_t, cos, sin)


CONV_PAD = 32


def _conv_kernel(buf_ref, u_ref, w_ref, bdw_ref, lg_ref, lb_ref, o_ref, nb_ref, full_ref, y_ref,
                 *, tt, n_t, col_chunk):
    lo = CONV_PAD - CONV_HALO
    if n_t > 1:
        t = pl.program_id(1)

        @pl.when(t == 0)
        def _():
            full_ref[:, lo:CONV_PAD, :] = buf_ref[...]

        @pl.when(t > 0)
        def _():
            full_ref[:, lo:CONV_PAD, :] = full_ref[:, tt + lo:tt + CONV_PAD, :]
    else:
        full_ref[:, lo:CONV_PAD, :] = buf_ref[...]
    full_ref[:, CONV_PAD:CONV_PAD + tt, :] = u_ref[...]

    if n_t > 1:
        @pl.when(t == n_t - 1)
        def _():
            nb_ref[...] = full_ref[:, tt + lo:tt + CONV_PAD, :]
    else:
        nb_ref[...] = full_ref[:, tt + lo:tt + CONV_PAD, :]

    d = u_ref.shape[-1]
    if tt % TILE_ROWS == 0:
        full_ref[:, CONV_PAD + tt:, :] = jnp.zeros((full_ref.shape[0], TILE_ROWS, d), F32)
        for c0 in range(0, d, col_chunk):
            cols = slice(c0, c0 + col_chunk)
            acc = None
            for rem in range(TILE_ROWS):
                z = None
                for k in range(CONV_WIDTH):
                    if (lo + k) % TILE_ROWS != rem:
                        continue
                    base = lo + k - rem
                    term = w_ref[k:k + 1, cols] * full_ref[:, base:base + tt + TILE_ROWS, cols]
                    z = term if z is None else z + term
                z = z[:, rem:rem + tt, :]
                acc = z if acc is None else acc + z
            y_ref[:, :, cols] = acc + bdw_ref[:, cols]
    else:
        for c0 in range(0, d, col_chunk):
            cols = slice(c0, c0 + col_chunk)
            acc = None
            for k in range(CONV_WIDTH):
                term = w_ref[k:k + 1, cols] * full_ref[:, lo + k:lo + k + tt, cols]
                acc = term if acc is None else acc + term
            y_ref[:, :, cols] = acc + bdw_ref[:, cols]

    y = y_ref[...]
    mu = jnp.mean(y, axis=-1, keepdims=True)
    yc = y - mu
    z = yc * lax.rsqrt(jnp.mean(yc * yc, axis=-1, keepdims=True) + EPS) * lg_ref[...] + lb_ref[...]
    o_ref[...] = z * _sigmoid(z)


def conv_ln(buf, u, wdw, bdw, ln_g, ln_b, *, bb, tt, n_t=1):
    b, _, d = buf.shape
    aligned = tt % TILE_ROWS == 0
    kern = functools.partial(_conv_kernel, tt=tt, n_t=n_t, col_chunk=128 if aligned else 256)
    full_rows = CONV_PAD + tt + (TILE_ROWS if aligned else 0)
    vec = lambda i, j: (0, 0)
    tile = lambda i, j: (i * n_t + j, 0, 0)
    halo = pl.BlockSpec((bb, CONV_HALO, d), lambda i, j: (i, 0, 0))
    return pl.pallas_call(
        kern,
        out_shape=(jax.ShapeDtypeStruct((b * n_t, tt, d), F32), jax.ShapeDtypeStruct((b, CONV_HALO, d), F32)),
        grid=(b // bb, n_t),
        in_specs=[
            halo,
            pl.BlockSpec((bb, tt, d), tile),
            pl.BlockSpec((CONV_WIDTH, d), vec),
            pl.BlockSpec((1, d), vec),
            pl.BlockSpec((1, d), vec),
            pl.BlockSpec((1, d), vec),
        ],
        out_specs=(pl.BlockSpec((bb, tt, d), tile), halo),
        scratch_shapes=[pltpu.VMEM((bb, full_rows, d), F32), pltpu.VMEM((bb, tt, d), F32)],
        compiler_params=_params("parallel", "arbitrary"),
        name="conv_ln",
    )(buf, u, wdw, bdw.reshape(1, d), ln_g.reshape(1, d), ln_b.reshape(1, d))


def _pos_term_kernel(pe_ref, w_ref, o_ref):
    o_ref[0] = _dot(pe_ref[0], w_ref[0])


def pos_term(pe, w1):
    n = pe.shape[0]
    k = CMP_LEN * HEAD_DIM
    pe8 = jnp.broadcast_to(pe.reshape(n, 1, k), (n, 8, k)).astype(BF)
    return pl.pallas_call(
        _pos_term_kernel,
        out_shape=jax.ShapeDtypeStruct((n, 8, CMP_HIDDEN), F32),
        grid=(n,),
        in_specs=[pl.BlockSpec((1, 8, k), lambda i: (i, 0, 0)),
                  pl.BlockSpec((1, k, CMP_HIDDEN), lambda i: (i, 0, 0))],
        out_specs=pl.BlockSpec((1, 8, CMP_HIDDEN), lambda i: (i, 0, 0)),
        compiler_params=_params("parallel"),
        name="pos_term",
    )(pe8, w1.reshape(n, k, CMP_HIDDEN).astype(BF))


def _chunk_matrix(x, g):
    cols = [x[:, s * KV_COLS + g * HEAD_DIM: s * KV_COLS + (g + 1) * HEAD_DIM] for s in range(CMP_STRIDE)]
    return jnp.concatenate(cols, axis=1)


TILES_PER_CHUNK = CMP_STRIDE * KV_HEADS // TILE_ROWS
assert 2 * KV_HEADS == TILE_ROWS


def _page_chunk_rows(tile_ref):
    n_chunks = tile_ref.shape[1] // TILES_PER_CHUNK
    low = lax.broadcasted_iota(jnp.int32, (TILE_ROWS, HEAD_DIM), 0) < KV_HEADS
    cols = []
    for s in range(CMP_STRIDE):
        pieces = []
        for c in range(0, n_chunks, 2):
            a = tile_ref[0, TILES_PER_CHUNK * c + s // 2]
            b = tile_ref[0, TILES_PER_CHUNK * (c + 1) + s // 2]
            if s % 2 == 0:
                pieces.append(jnp.where(low, a, pltpu.roll(b, KV_HEADS, axis=0)))
            else:
                pieces.append(jnp.where(low, pltpu.roll(a, KV_HEADS, axis=0), b))
        cols.append(jnp.concatenate(pieces, axis=0))
    return jnp.concatenate(cols, axis=1)


def _compress_mlp(x, shift, w1_ref, pt_ref, w2_ref, b2_ref):
    n = x.shape[0]
    fs = _dot(x, w1_ref[0])
    first = fs[:, :CMP_HIDDEN]
    second = pltpu.roll(fs[:, CMP_HIDDEN:], n - shift, axis=0)
    pre = first + second + pt_ref[0, 0:1, :]
    hid = (pre * _sigmoid(pre)).astype(BF)
    out = _dot(hid, w2_ref[0]) + b2_ref[0]
    row = lax.broadcasted_iota(jnp.int32, (n, 1), 0)
    return jnp.where(row < n - shift, out, 0.0)


def _compress_rows_kernel(k_ref, v_ref, w1_ref, pt_ref, w2_ref, b2_ref, ok_ref, ov_ref):
    for i, (x_ref, o_ref) in enumerate(((k_ref, ok_ref), (v_ref, ov_ref))):
        w = (w1_ref.at[i:i + 1], pt_ref.at[i:i + 1], w2_ref.at[i:i + 1], b2_ref.at[i:i + 1])
        x = x_ref[0]
        for g in range(KV_HEADS):
            out = _compress_mlp(_chunk_matrix(x, g), 1, *w)
            o_ref[0, :, g * HEAD_DIM:(g + 1) * HEAD_DIM] = out.astype(o_ref.dtype)


def _compress_pages_kernel(*refs, n_pages):
    refs = refs[1:]
    w1_ref, pt_ref, w2_ref, b2_ref, ok_ref, ov_ref = refs[2 * n_pages:]
    for i, o_ref in enumerate((ok_ref, ov_ref)):
        w = (w1_ref.at[i:i + 1], pt_ref.at[i:i + 1], w2_ref.at[i:i + 1], b2_ref.at[i:i + 1])
        pages = refs[i * n_pages:(i + 1) * n_pages]
        x = jnp.concatenate([_page_chunk_rows(p).astype(BF) for p in pages], axis=0)
        o_ref[0] = _compress_mlp(x, KV_HEADS, *w).astype(o_ref.dtype)


def _compress_weight_specs():
    fixed3 = lambda *a: (0, 0, 0)
    return [
        pl.BlockSpec((2, CMP_STRIDE * HEAD_DIM, 2 * CMP_HIDDEN), fixed3),
        pl.BlockSpec((2, 8, CMP_HIDDEN), fixed3),
        pl.BlockSpec((2, CMP_HIDDEN, HEAD_DIM), fixed3),
        pl.BlockSpec((2, 1, HEAD_DIM), fixed3),
    ]


def compress_rows(k_rows, v_rows, cw):
    b, n, width = k_rows.shape
    spec = pl.BlockSpec((1, n, width), lambda i: (i, 0, 0))
    out = jax.ShapeDtypeStruct((b, n, KV_COLS), BF)
    ospec = pl.BlockSpec((1, n, KV_COLS), lambda i: (i, 0, 0))
    return pl.pallas_call(
        _compress_rows_kernel,
        out_shape=(out, out),
        grid=(b,),
        in_specs=[spec, spec] + _compress_weight_specs(),
        out_specs=(ospec, ospec),
        compiler_params=_params("parallel"),
        name="compress_rows",
    )(k_rows, v_rows, *cw)


def compress_pages(page_table, n_pages, cache_k, cache_v, cw):
    n_pool, tiles, rows, hd = cache_k.shape
    b = page_table.shape[0] // n_pages
    n = n_pages * tiles // TILES_PER_CHUNK * KV_HEADS

    def page_spec(p):
        return pl.BlockSpec((1, tiles, rows, hd), lambda i, pt: (pt[i * n_pages + p], 0, 0, 0))

    pages = [page_spec(p) for p in range(n_pages)]
    out = jax.ShapeDtypeStruct((b, n, HEAD_DIM), BF)
    ospec = pl.BlockSpec((1, n, HEAD_DIM), lambda i, pt: (i, 0, 0))
    grid_spec = pltpu.PrefetchScalarGridSpec(
        num_scalar_prefetch=1,
        grid=(b,),
        in_specs=pages + pages + _compress_weight_specs(),
        out_specs=(ospec, ospec),
    )
    return pl.pallas_call(
        functools.partial(_compress_pages_kernel, n_pages=n_pages),
        out_shape=(out, out),
        grid_spec=grid_spec,
        compiler_params=_params("parallel"),
        name="compress_pages",
    )(page_table, *([cache_k] * n_pages), *([cache_v] * n_pages), *cw)


def _softmax_cols(s, mask):
    s = jnp.where(mask, s, NEG_INF)
    m = jnp.max(s, axis=0, keepdims=True)
    m = jnp.where(m == NEG_INF, 0.0, m)
    p = jnp.where(mask, jnp.exp2(s - m), 0.0)
    return p / jnp.maximum(jnp.sum(p, axis=0, keepdims=True), 1e-30)


def _top_rank_lanes(score, valid, n):
    idx = lax.broadcasted_iota(jnp.int32, score.shape, 1)
    cnt = jnp.zeros(score.shape, F32)
    for j in range(n):
        other = score[:, j:j + 1]
        tie = jnp.where(idx > j, 1.0, 0.0)
        cnt = cnt + jnp.where(other > score, 1.0, jnp.where(other == score, tie, 0.0))
    return jnp.where(cnt < TOP_N, jnp.where(valid, 1.0, 0.0), 0.0)


def _top_rank_sublanes(score, valid):
    n = score.shape[0]
    groups = [score[v:v + 8, :] for v in range(0, n, 8)]
    cnts = [jnp.zeros(g.shape, F32) for g in groups]
    for j in range(n):
        other = score[j:j + 1, :]
        for v, g in enumerate(groups):
            if 8 * v > j:
                beats = other >= g
            elif 8 * v + 7 <= j:
                beats = other > g
            else:
                sub = 8 * v + lax.broadcasted_iota(jnp.int32, g.shape, 0)
                beats = (other > g) | ((other == g) & (sub > j))
            cnts[v] = cnts[v] + jnp.where(beats, 1.0, 0.0)
    cnt = jnp.concatenate(cnts, axis=0)
    return jnp.where(cnt < TOP_N, jnp.where(valid, 1.0, 0.0), 0.0)


def _flash_update(s, vt, carry):
    m, l, acc = carry
    m_new = jnp.maximum(m, jnp.max(s, axis=0, keepdims=True))
    m_safe = jnp.where(m_new == NEG_INF, 0.0, m_new)
    alpha = jnp.exp2(m - m_safe)
    p = jnp.exp2(s - m_safe)
    l = alpha * l + jnp.sum(p, axis=0, keepdims=True)
    acc = alpha * acc + _dot(vt, p.astype(BF))
    return m_new, l, acc


def _prompt_attn_kernel(q_ref, gt_ref, kc_ref, vct_ref, ks_ref, vst_ref, kw_ref, vwt_ref,
                        o_ref, sel_ref, *, n_cmp):
    qt = pl.program_id(2)
    n_rows = GROUP * Q_TILE
    q = q_ref[0]
    q_rows = jnp.concatenate([q[:, r * HEAD_DIM:(r + 1) * HEAD_DIM] for r in range(GROUP)], axis=0)
    t0 = qt * Q_TILE
    pos_t = t0 + lax.broadcasted_iota(jnp.int32, (1, Q_TILE), 1)
    pos_rows = jnp.concatenate([pos_t] * GROUP, axis=1)

    n_cpad = kc_ref.shape[1]
    ci = lax.broadcasted_iota(jnp.int32, (n_cpad, 1), 0)
    cmask = (ci * CMP_STRIDE + (CMP_LEN - 1) <= pos_rows) & (ci < n_cmp)
    p_cmp = _softmax_cols(_dot_nt(kc_ref[0], q_rows), cmask)
    o_cmp = _dot(vct_ref[0, 0], p_cmp.astype(BF))

    n_steps = ks_ref.shape[2]
    blk_per_step = KEY_STEP // SEL_BLOCK
    n_sel = n_steps * blk_per_step
    p_sum = p_cmp[:, 0:Q_TILE]
    for r in range(1, GROUP):
        p_sum = p_sum + p_cmp[:, r * Q_TILE:(r + 1) * Q_TILE]
    bj = lax.broadcasted_iota(jnp.int32, (n_sel, n_cpad), 0) * SEL_BLOCK
    bi = lax.broadcasted_iota(jnp.int32, (n_sel, n_cpad), 1) * CMP_STRIDE
    overlap_t = jnp.where((bi < bj + SEL_BLOCK) & (bi + CMP_LEN > bj), 1.0, 0.0).astype(BF)
    hi, mid, lo = _split3(p_sum)
    imp = _dot(overlap_t, hi) + _dot(overlap_t, mid) + _dot(overlap_t, lo)
    blk = lax.broadcasted_iota(jnp.int32, (n_sel, 1), 0)
    cur = pos_t >> SEL_SHIFT
    forced = (blk == 0) | (blk == cur) | (blk == cur - 1)
    causal = blk * SEL_BLOCK <= pos_t
    score = jnp.where(causal, imp + FORCE_BONUS * jnp.where(forced, 1.0, 0.0), NEG_INF)
    sel = _top_rank_sublanes(score, causal)
    for st in range(n_steps):
        sel_ref[st] = sel[st * blk_per_step:(st + 1) * blk_per_step, :]

    init = (jnp.full((1, n_rows), NEG_INF, F32), jnp.zeros((1, n_rows), F32),
            jnp.zeros((HEAD_DIM, n_rows), F32))

    key_in_blk = lax.broadcasted_iota(jnp.int32, (SEL_BLOCK, 1), 0)

    def sel_scores(st):
        picked = sel_ref[st]
        pieces = []
        for h in range(blk_per_step):
            kpos = st * KEY_STEP + h * SEL_BLOCK + key_in_blk
            ok = (kpos <= pos_t) & (picked[h:h + 1, :] > 0.5)
            pieces.append(jnp.where(ok, 0.0, NEG_INF))
        b = jnp.concatenate(pieces, axis=0)
        return _dot_nt(ks_ref[0, 0, st], q_rows) + jnp.concatenate([b] * GROUP, axis=1)

    def sel_step(st, carry):
        return _flash_update(sel_scores(st), vst_ref[0, 0, st], carry)

    n_needed = (t0 + Q_TILE + KEY_STEP - 1) // KEY_STEP
    _, l_sel, acc_sel = lax.fori_loop(0, n_needed, sel_step, init)
    o_sel = acc_sel / jnp.maximum(l_sel, 1e-30)

    n_wt = (WINDOW + Q_TILE) // KEY_TILE
    first = qt * (Q_TILE // KEY_TILE) - WINDOW // KEY_TILE
    tiles = [jnp.maximum(first + i, 0) for i in range(n_wt)]
    k_win = jnp.concatenate([kw_ref[0, 0, kt] for kt in tiles], axis=0)
    vt_win = jnp.concatenate([vwt_ref[0, 0, kt] for kt in tiles], axis=1)
    kpos = first * KEY_TILE + lax.broadcasted_iota(jnp.int32, (n_wt * KEY_TILE, 1), 0)
    ok = (kpos >= 0) & (kpos <= pos_rows) & (pos_rows - kpos < WINDOW)
    s_win = _dot_nt(k_win, q_rows) + jnp.where(ok, 0.0, NEG_INF)
    _, l_win, acc_win = _flash_update(s_win, vt_win, init)
    o_win = acc_win / jnp.maximum(l_win, 1e-30)

    gt = gt_ref[0]
    for r in range(GROUP):
        cols = slice(r * Q_TILE, (r + 1) * Q_TILE)
        base = r * N_BRANCH
        mix = (gt[base:base + 1, :] * o_cmp[:, cols] + gt[base + 1:base + 2, :] * o_sel[:, cols]
               + gt[base + 2:base + 3, :] * o_win[:, cols])
        o_ref[0, :, r * HEAD_DIM:(r + 1) * HEAD_DIM] = mix.T.astype(o_ref.dtype)


def prompt_attention(q, gates_t, kc, vct, ks, vst, kw, vwt, n_cmp):
    b = ks.shape[0]
    t = ks.shape[2] * KEY_STEP
    q = q.reshape(-1, Q_TILE, q.shape[-1])
    n_q = t // Q_TILE
    n_steps = ks.shape[2]
    n_kt = kw.shape[2]
    n_cpad = kc.shape[1]
    whole = lambda i, g, j: (i, g, 0, 0, 0)
    width = GROUP * HEAD_DIM
    return pl.pallas_call(
        functools.partial(_prompt_attn_kernel, n_cmp=n_cmp),
        out_shape=jax.ShapeDtypeStruct((b, t, N_HEADS * HEAD_DIM), BF),
        grid=(b, KV_HEADS, n_q),
        in_specs=[
            pl.BlockSpec((1, Q_TILE, width), lambda i, g, j: (i * n_q + j, 0, g)),
            pl.BlockSpec((1, GROUP * N_BRANCH, Q_TILE), lambda i, g, j: (g, 0, i * n_q + j)),
            pl.BlockSpec((1, n_cpad, HEAD_DIM), lambda i, g, j: (i, 0, g)),
            pl.BlockSpec((1, 1, HEAD_DIM, n_cpad), lambda i, g, j: (i, g, 0, 0)),
            pl.BlockSpec((1, 1, n_steps, KEY_STEP, HEAD_DIM), whole),
            pl.BlockSpec((1, 1, n_steps, HEAD_DIM, KEY_STEP), whole),
            pl.BlockSpec((1, 1, n_kt, KEY_TILE, HEAD_DIM), whole),
            pl.BlockSpec((1, 1, n_kt, HEAD_DIM, KEY_TILE), whole),
        ],
        out_specs=pl.BlockSpec((1, Q_TILE, width), lambda i, g, j: (i, j, g)),
        scratch_shapes=[pltpu.VMEM((n_steps, KEY_STEP // SEL_BLOCK, Q_TILE), F32)],
        compiler_params=_params("parallel", "parallel", "arbitrary"),
        name="prompt_attention",
    )(q, gates_t, kc, vct, ks, vst, kw, vwt)


def _softmax_two(s_a, bias_a, s_b, bias_b):
    s_a = s_a + bias_a
    s_b = s_b + bias_b
    m = jnp.maximum(jnp.max(s_a, axis=1, keepdims=True), jnp.max(s_b, axis=1, keepdims=True))
    m = jnp.where(m == NEG_INF, 0.0, m)
    p_a = jnp.exp2(s_a - m)
    p_b = jnp.exp2(s_b - m)
    total = jnp.sum(p_a, axis=1, keepdims=True) + jnp.sum(p_b, axis=1, keepdims=True)
    return p_a, p_b, 1.0 / jnp.maximum(total, 1e-30)


def _sample_attn_kernel(*refs, n_pages, t_new, past_len, n_cmp):
    refs = refs[1:]
    q_ref, gate_ref, kc_ref, vc_ref = refs[:4]
    ks_pages = refs[4:4 + n_pages]
    vs_pages = refs[4 + n_pages:4 + 2 * n_pages]
    (ksn_ref, vsn_ref, kwp_ref, vwp_ref, kwn_ref, vwn_ref, o_ref,
     expand_ref, past_bias_ref, win_bias_ref) = refs[4 + 2 * n_pages:]

    n_rows = q_ref.shape[1]
    rows_per_head = n_rows // KV_HEADS
    q = q_ref[0]
    row = lax.broadcasted_iota(jnp.int32, (n_rows, 1), 0)
    row_head = _div(row, rows_per_head)
    pos = past_len + _mod(row, t_new)
    n_past = n_pages * ks_pages[0].shape[1]
    n_win = kwp_ref.shape[1]

    def col_key(n_cols):
        return _div(lax.broadcasted_iota(jnp.int32, (1, n_cols), 1), KV_HEADS)

    def head_bias(n_cols):
        col = lax.broadcasted_iota(jnp.int32, (1, n_cols), 1)
        return jnp.where(_mod(col, KV_HEADS) == row_head, 0.0, NEG_INF)

    @pl.when(pl.program_id(0) == 0)
    def _():
        ej = lax.broadcasted_iota(jnp.int32, (128, n_past), 0)
        ek = _div(lax.broadcasted_iota(jnp.int32, (128, n_past), 1), KV_HEADS)
        expand_ref[...] = jnp.where((ek >> SEL_SHIFT) == ej, 1.0, 0.0).astype(BF)
        past_bias_ref[...] = jnp.where(col_key(n_past) <= pos, head_bias(n_past), NEG_INF)
        wpos = past_len - n_win // KV_HEADS + col_key(n_win)
        win_bias_ref[...] = jnp.where((wpos <= pos) & (pos - wpos < WINDOW), head_bias(n_win), NEG_INF)

    new_pos = past_len + col_key(128)
    new_bias = jnp.where((col_key(128) < t_new) & (new_pos <= pos), head_bias(128), NEG_INF)

    def pad_new(ref):
        x = ref[0]
        return jnp.concatenate([x, jnp.zeros((128 - x.shape[0], x.shape[1]), x.dtype)], axis=0).astype(BF)

    n_c = kc_ref.shape[1]
    chunk = col_key(n_c)
    c_ok = (chunk * CMP_STRIDE + (CMP_LEN - 1) <= pos) & (chunk < n_cmp)
    s_c = _dot_nt(q, kc_ref[0]) + jnp.where(c_ok, head_bias(n_c), NEG_INF)
    m_c = jnp.max(s_c, axis=1, keepdims=True)
    m_c = jnp.where(m_c == NEG_INF, 0.0, m_c)
    p_c = jnp.exp2(s_c - m_c)
    p_c = p_c / jnp.maximum(jnp.sum(p_c, axis=1, keepdims=True), 1e-30)
    o_cmp = _dot(p_c.astype(BF), vc_ref[0])

    ra = lax.broadcasted_iota(jnp.int32, (n_rows, n_rows), 0)
    rb = lax.broadcasted_iota(jnp.int32, (n_rows, n_rows), 1)
    same_gt = ((_div(ra, rows_per_head) == _div(rb, rows_per_head))
               & (_mod(ra, t_new) == _mod(rb, t_new)))
    group_sum = jnp.where(same_gt, 1.0, 0.0).astype(BF)
    hi, mid, lo = _split3(p_c)
    p_sum = _dot(group_sum, hi) + _dot(group_sum, mid) + _dot(group_sum, lo)
    n_keys = past_len + t_new
    n_sel = -(-n_keys // SEL_BLOCK)
    bi = _div(lax.broadcasted_iota(jnp.int32, (n_c, 128), 0), KV_HEADS) * CMP_STRIDE
    bj = lax.broadcasted_iota(jnp.int32, (n_c, 128), 1) * SEL_BLOCK
    overlap = jnp.where((bi < bj + SEL_BLOCK) & (bi + CMP_LEN > bj), 1.0, 0.0).astype(BF)
    imp = _dot01(p_sum, overlap)
    blk = lax.broadcasted_iota(jnp.int32, (1, 128), 1)
    cur = pos >> SEL_SHIFT
    forced = (blk == 0) | (blk == cur) | (blk == cur - 1)
    causal = (blk * SEL_BLOCK <= pos) & (blk < n_sel)
    score = jnp.where(causal, imp + FORCE_BONUS * jnp.where(forced, 1.0, 0.0), NEG_INF)
    sel = _top_rank_lanes(score, causal, n_sel)

    k_past = jnp.concatenate([p[0].astype(BF) for p in ks_pages], axis=0)
    v_past = jnp.concatenate([p[0].astype(BF) for p in vs_pages], axis=0)
    sel16 = sel.astype(BF)
    bias_past = jnp.where(_dot(sel16, expand_ref[...]) > 0.5, past_bias_ref[...], NEG_INF)
    nj = lax.broadcasted_iota(jnp.int32, (128, 128), 0)
    nk = past_len + _div(lax.broadcasted_iota(jnp.int32, (128, 128), 1), KV_HEADS)
    expand_new = jnp.where((nk >> SEL_SHIFT) == nj, 1.0, 0.0).astype(BF)
    bias_new = jnp.where(_dot(sel16, expand_new) > 0.5, new_bias, NEG_INF)
    p_a, p_b, inv = _softmax_two(_dot_nt(q, k_past), bias_past, _dot_nt(q, pad_new(ksn_ref)), bias_new)
    o_sel = (_dot(p_a.astype(BF), v_past) + _dot(p_b.astype(BF), pad_new(vsn_ref))) * inv

    bias_wn = jnp.where(pos - new_pos < WINDOW, new_bias, NEG_INF)
    p_a, p_b, inv = _softmax_two(_dot_nt(q, kwp_ref[0].astype(BF)), win_bias_ref[...],
                                 _dot_nt(q, pad_new(kwn_ref)), bias_wn)
    o_win = (_dot(p_a.astype(BF), vwp_ref[0].astype(BF)) + _dot(p_b.astype(BF), pad_new(vwn_ref))) * inv

    gate = gate_ref[0]
    o = gate[:, 0:1] * o_cmp + gate[:, 1:2] * o_sel + gate[:, 2:3] * o_win
    o_ref[0] = o.astype(o_ref.dtype)


def sample_attention(page_table, q, gates, kc, vc, cache_ks, cache_vs, ks_new, vs_new,
                     kw_past, vw_past, kw_new, vw_new, *, t_new, n_cmp):
    b, n_rows, _ = q.shape
    n_pages = page_table.shape[0] // b
    page_rows = cache_ks.shape[1]
    n_c = kc.shape[1]
    n_win = kw_past.shape[1]
    n_new = ks_new.shape[1]
    n_past = n_pages * page_rows

    def page_spec(p):
        return pl.BlockSpec((1, page_rows, HEAD_DIM), lambda i, pt: (pt[i * n_pages + p], 0, 0))

    per_seq = lambda rows, cols=HEAD_DIM: pl.BlockSpec((1, rows, cols), lambda i, pt: (i, 0, 0))
    pages = [page_spec(p) for p in range(n_pages)]
    grid_spec = pltpu.PrefetchScalarGridSpec(
        num_scalar_prefetch=1,
        grid=(b,),
        in_specs=[per_seq(n_rows), per_seq(n_rows, N_BRANCH), per_seq(n_c), per_seq(n_c)]
        + pages + pages
        + [per_seq(n_new), per_seq(n_new), per_seq(n_win), per_seq(n_win), per_seq(n_new), per_seq(n_new)],
        out_specs=per_seq(n_rows),
        scratch_shapes=[pltpu.VMEM((128, n_past), BF), pltpu.VMEM((n_rows, n_past), F32),
                        pltpu.VMEM((n_rows, n_win), F32)],
    )
    kern = functools.partial(_sample_attn_kernel, n_pages=n_pages, t_new=t_new,
                             past_len=n_past // KV_HEADS, n_cmp=n_cmp)
    return pl.pallas_call(
        kern,
        out_shape=jax.ShapeDtypeStruct((b, n_rows, HEAD_DIM), BF),
        grid_spec=grid_spec,
        compiler_params=_params("arbitrary"),
        name="sample_attention",
    )(page_table, q, gates, kc, vc, *([cache_ks] * n_pages), *([cache_vs] * n_pages),
      ks_new, vs_new, kw_past, vw_past, kw_new, vw_new)


def _window_update_kernel(k_ref, v_ref, kn_ref, vn_ref, ok_ref, ov_ref):
    n = k_ref.shape[1]
    n_new = kn_ref.shape[1]
    for s_ref, n_ref, o_ref in ((k_ref, kn_ref, ok_ref), (v_ref, vn_ref, ov_ref)):
        o_ref[:, 0:n - n_new, :] = s_ref[:, n_new:n, :]
        o_ref[:, n - n_new:n, :] = n_ref[...]


def window_update(k_state, v_state, k_new, v_new):
    b, n, d = k_state.shape
    n_new = k_new.shape[1]
    bb = 2
    state = pl.BlockSpec((bb, n, d), lambda i: (i, 0, 0))
    new = pl.BlockSpec((bb, n_new, d), lambda i: (i, 0, 0))
    out = jax.ShapeDtypeStruct((b, n, d), F32)
    return pl.pallas_call(
        _window_update_kernel,
        out_shape=(out, out),
        grid=(b // bb,),
        in_specs=[state, state, new, new],
        out_specs=(state, state),
        compiler_params=_params("parallel"),
        name="window_update",
    )(k_state, v_state, k_new, v_new)


def _rope_tables(pos):
    half = HEAD_DIM // 2
    inv = ROPE_THETA ** (-jnp.arange(half, dtype=F32) / half)
    ang = pos.astype(F32)[:, None] * inv[None, :]
    cos = jnp.cos(ang)
    sin = jnp.sin(ang)
    return jnp.concatenate([cos, cos], axis=1), jnp.concatenate([-sin, sin], axis=1)


def kernel(x_prompt, x_sample, cache_k_cmp, cache_v_cmp, cache_k_sel, cache_v_sel, state_k_win, state_v_win, state_conv, page_table, norm_g, conv_w1, conv_b1, conv_wdw, conv_bdw, conv_ln_g, conv_ln_b, conv_w2, conv_b2, kv_norm_g, w_kv, cmp_w1, cmp_pe, cmp_w2, cmp_b2, nsa_w_qg, nsa_w_o, ffn_w_in, ffn_w_out):
    bp, tp, d = x_prompt.shape
    bs, ts, _ = x_sample.shape
    n_p = bp * tp
    n_s = bs * ts
    n_pool, page, _, _ = cache_k_sel.shape
    n_pages = page_table.shape[1]
    past_len = n_pages * page
    bf = lambda w: w.astype(BF)

    def split(a):
        return a[:n_p], a[n_p:]

    x_in = (x_prompt.reshape(n_p, d), x_sample.reshape(n_s, d))
    pos = jnp.concatenate([jnp.tile(jnp.arange(tp, dtype=jnp.int32), bp),
                           jnp.tile(past_len + jnp.arange(ts, dtype=jnp.int32), bs)])
    cos, sin = _rope_tables(pos)

    g = norm_g[0]
    u = glu_in(x_in, g[0], bf(conv_w1[0]), conv_b1[0])
    conv_tile = 128
    assert tp % conv_tile == 0 and (n_p + n_s) % conv_tile == 0
    u_s = u[n_p:].reshape(bs, ts, d)
    conv_args = (conv_wdw[0], conv_bdw[0], conv_ln_g[0], conv_ln_b[0])
    v_p, conv_p = conv_ln(jnp.zeros((bp, CONV_HALO, d), F32), u.reshape(-1, conv_tile, d), *conv_args,
                          bb=1, tt=conv_tile, n_t=tp // conv_tile)
    v_s, conv_s = conv_ln(state_conv[0], u_s, *conv_args, bb=16, tt=ts)
    x = mm_norm_res((v_p.reshape(n_p, d), v_s.reshape(n_s, d)), bf(conv_w2[0]), conv_b2[0], g[1], x_in)
    ffn_in16, ffn_out16 = bf(ffn_w_in), bf(ffn_w_out)
    x = ffn(x, g[2], ffn_in16, ffn_out16, g[3], 0)
    conv_p, conv_s = conv_p[None], conv_s[None]

    kv32, kv16 = kv_proj(x, kv_norm_g, bf(w_kv), cos, sin)
    rows_p = [kv32[s, :n_p].reshape(bp, tp, KV_HEADS, HEAD_DIM) for s in range(N_KV_SLOTS)]
    rows_s = [kv32[s, n_p:].reshape(bs, ts, KV_HEADS, HEAD_DIM) for s in range(N_KV_SLOTS)]
    n_wp = min(WINDOW, tp)
    kw_p, vw_p = rows_p[4][:, tp - n_wp:], rows_p[5][:, tp - n_wp:]

    def new_rows(slot):
        return kv32[slot, n_p:].reshape(bs, ts * KV_HEADS, HEAD_DIM)

    head_rows = lambda a: a.reshape(a.shape[0], -1, HEAD_DIM)
    kw_s, vw_s = window_update(head_rows(state_k_win), head_rows(state_v_win), new_rows(4), new_rows(5))
    kw_s, vw_s = kw_s.reshape(state_k_win.shape), vw_s.reshape(state_v_win.shape)

    w1 = cmp_w1.reshape(2, 2, CMP_STRIDE * HEAD_DIM, CMP_HIDDEN)
    w1cat = bf(jnp.concatenate([w1[:, 0], w1[:, 1]], axis=-1))
    cw = (w1cat, pos_term(cmp_pe, cmp_w1), bf(cmp_w2), cmp_b2.reshape(2, 1, HEAD_DIM))
    chunk_cols = CMP_STRIDE * KV_COLS
    kc_p, vc_p = compress_rows(kv16[0, :n_p].reshape(bp, tp // CMP_STRIDE, chunk_cols),
                               kv16[1, :n_p].reshape(bp, tp // CMP_STRIDE, chunk_cols), cw)
    pt_flat = page_table.reshape(-1)
    assert (past_len + ts) // CMP_STRIDE == past_len // CMP_STRIDE
    page_rows = page * KV_HEADS
    as_tiles = lambda c: c.reshape(n_pool, page_rows // TILE_ROWS, TILE_ROWS, HEAD_DIM)
    kc_s, vc_s = compress_pages(pt_flat, n_pages, as_tiles(cache_k_cmp), as_tiles(cache_v_cmp), cw)

    g = norm_g[1]
    w_qg = nsa_w_qg[0]
    n_q_cols = N_HEADS * HEAD_DIM
    n_gate = N_HEADS * N_BRANCH
    wg = w_qg[:, n_q_cols:]
    wg_pad = bf(jnp.pad(wg, ((0, 0), (0, 128 - n_gate))))
    q, gates, gates_t = q_proj(x, g[0], bf(w_qg[:, :n_q_cols]), wg_pad, bf(wg.T), cos, sin)

    assert tp % KEY_STEP == 0 and tp % Q_TILE == 0

    def key_tiles(a, tk):
        return a.reshape(bp, tp // tk, tk, KV_HEADS, HEAD_DIM).transpose(0, 3, 1, 2, 4)

    def value_tiles_t(a, tk):
        return a.reshape(bp, tp // tk, tk, KV_HEADS, HEAD_DIM).transpose(0, 3, 1, 4, 2)

    n_cmp_p = tp // CMP_STRIDE - 1
    vct_p = vc_p.reshape(bp, tp // CMP_STRIDE, KV_HEADS, HEAD_DIM).transpose(0, 2, 3, 1)
    gates_t_p = gates_t[:, :n_p].reshape(KV_HEADS, GROUP * N_BRANCH, n_p)
    assert (n_p + n_s) % Q_TILE == 0
    o_p = prompt_attention(q, gates_t_p, kc_p, vct_p,
                           key_tiles(kv16[2, :n_p], KEY_STEP), value_tiles_t(kv16[3, :n_p], KEY_STEP),
                           key_tiles(kv16[4, :n_p], KEY_TILE), value_tiles_t(kv16[5, :n_p], KEY_TILE), n_cmp_p)

    def seq_rows(a, width):
        return a.reshape(bs, ts, KV_HEADS, GROUP, width).transpose(0, 2, 3, 1, 4).reshape(bs, -1, width)

    q_s = seq_rows(q[n_p:], HEAD_DIM)
    gates_s = seq_rows(gates[n_p:, :n_gate], N_BRANCH)

    o_s = sample_attention(
        pt_flat, q_s, gates_s, kc_s, vc_s,
        head_rows(cache_k_sel), head_rows(cache_v_sel), new_rows(2), new_rows(3),
        head_rows(state_k_win), head_rows(state_v_win), new_rows(4), new_rows(5), t_new=ts, n_cmp=past_len // CMP_STRIDE - 1)
    o_s = o_s.reshape(bs, KV_HEADS, GROUP, ts, HEAD_DIM).transpose(0, 3, 1, 2, 4).reshape(n_s, n_q_cols)
    x = mm_norm_res((o_p.reshape(n_p, n_q_cols), o_s), bf(nsa_w_o[0]), None, g[1], x)
    x = ffn(x, g[2], ffn_in16, ffn_out16, g[3], 1)

    y_p, y_s = split(x)
    return (y_p.reshape(bp, tp, d), y_s.reshape(bs, ts, d), conv_p,
            rows_p[0], rows_p[1], rows_p[2], rows_p[3], kw_p, vw_p,
            conv_s, rows_s[0], rows_s[1], rows_s[2], rows_s[3], kw_s, vw_s)
```
